```python
import jax, jax.numpy as jnp
from jax import lax
import numpy as np

D_MODEL = 2048
BATCH = 16
SEQ = 2048
DEPTH = 2

D_CONV = D_MODEL // 2
CONV_WIDTH = 31
D_POOL = D_MODEL // 2
POOL_WINDOWS = (2, 4, 8, 16)
POOL_GROUP = D_POOL // len(POOL_WINDOWS)
D_SHORT = D_MODEL
SHORT_WIDTH = 3
D_FF = -(-8 * D_MODEL // (3 * 256)) * 256
N_EVEN = (DEPTH + 1) // 2
N_ODD = DEPTH // 2
EPS = 1e-6

kernel_name = 'hybrid_conformer_pool_shortconv_block'


def rms_norm(x, g):
    xf = x.astype(jnp.float32)
    y = xf * lax.rsqrt(jnp.mean(xf * xf, axis=-1, keepdims=True) + EPS)
    return (y * g.astype(jnp.float32)).astype(x.dtype)


def layer_norm(x, g, b):
    xf = x.astype(jnp.float32)
    mu = jnp.mean(xf, axis=-1, keepdims=True)
    xc = xf - mu
    var = jnp.mean(xc * xc, axis=-1, keepdims=True)
    y = xc * lax.rsqrt(var + EPS) * g.astype(jnp.float32) + b.astype(jnp.float32)
    return y.astype(x.dtype)


def causal_depthwise_conv(x, w):
    k, c = w.shape
    return lax.conv_general_dilated(
        x, w[:, None, :].astype(x.dtype), window_strides=(1,),
        padding=[(k - 1, 0)], dimension_numbers=('NWC', 'WIO', 'NWC'),
        feature_group_count=c)


def multiscale_pool(v, w_pool, scale):
    b, s, _ = v.shape
    vf = v.astype(jnp.float32)
    cnt_pos = jnp.arange(s) + 1
    outs = []
    for g, w in enumerate(POOL_WINDOWS):
        xg = vf[..., g * POOL_GROUP:(g + 1) * POOL_GROUP]
        cs = jnp.cumsum(xg, axis=1)
        lag = jnp.pad(cs, ((0, 0), (w, 0), (0, 0)))[:, :s]
        cnt = jnp.minimum(cnt_pos, w).astype(jnp.float32)[None, :, None]
        outs.append((cs - lag) / cnt - xg)
    p = jnp.stack(outs, axis=2).astype(v.dtype)
    p = jnp.einsum('bsgc,gcd->bsgd', p, w_pool).reshape(b, s, D_POOL)
    return p * scale


def conv_pool_mixer(x, norm_g, w_in, conv_w, conv_b, ln_g, ln_b, w_pool, pool_scale, w_out):
    h = rms_norm(x, norm_g)
    u = h @ w_in
    a_val = u[..., :D_CONV]
    a_gate = u[..., D_CONV:2 * D_CONV]
    b_in = u[..., 2 * D_CONV:]
    a = a_val * jax.nn.sigmoid(a_gate)
    a = causal_depthwise_conv(a, conv_w) + conv_b
    a = jax.nn.silu(layer_norm(a, ln_g, ln_b))
    p = multiscale_pool(b_in, w_pool, pool_scale)
    return jnp.concatenate([a, p], axis=-1) @ w_out


def short_conv_mixer(x, norm_g, w_in, conv_w, w_out):
    h = rms_norm(x, norm_g)
    u = h @ w_in
    gate_b = u[..., :D_SHORT]
    gate_c = u[..., D_SHORT:2 * D_SHORT]
    v = u[..., 2 * D_SHORT:]
    y = gate_b * causal_depthwise_conv(gate_c * v, conv_w)
    return y @ w_out


def swiglu(h, w_gate, w_up, w_down):
    return (jax.nn.silu(h @ w_gate) * (h @ w_up)) @ w_down


def _normal(k, shape, fan_in):
    return jax.random.normal(k, shape, jnp.float32) * (fan_in ** -0.5)


def setup_inputs(seed: int = 0) -> dict:
    key = jax.random.key(seed)
    ks = jax.random.split(key, 20)
    d = D_MODEL
    x = jax.random.normal(ks[0], (BATCH, SEQ, d), jnp.float32)
    mix_norm_e = 1.0 + 0.02 * jax.random.normal(ks[1], (N_EVEN, d), jnp.float32)
    w_in_e = _normal(ks[2], (N_EVEN, d, 2 * D_CONV + D_POOL), d)
    conv_w_e = _normal(ks[3], (N_EVEN, CONV_WIDTH, D_CONV), CONV_WIDTH)
    conv_b_e = 0.02 * jax.random.normal(ks[4], (N_EVEN, D_CONV), jnp.float32)
    ln_g_e = 1.0 + 0.02 * jax.random.normal(ks[5], (N_EVEN, D_CONV), jnp.float32)
    ln_b_e = 0.02 * jax.random.normal(ks[6], (N_EVEN, D_CONV), jnp.float32)
    w_pool_e = _normal(ks[7], (N_EVEN, len(POOL_WINDOWS), POOL_GROUP, POOL_GROUP), POOL_GROUP)
    pool_scale_e = 1.0 + 0.02 * jax.random.normal(ks[8], (N_EVEN, D_POOL), jnp.float32)
    w_out_e = _normal(ks[9], (N_EVEN, D_CONV + D_POOL, d), D_CONV + D_POOL)
    mix_norm_o = 1.0 + 0.02 * jax.random.normal(ks[10], (N_ODD, d), jnp.float32)
    w_in_o = _normal(ks[11], (N_ODD, d, 3 * D_SHORT), d)
    conv_w_o = _normal(ks[12], (N_ODD, SHORT_WIDTH, D_SHORT), SHORT_WIDTH)
    w_out_o = _normal(ks[13], (N_ODD, D_SHORT, d), D_SHORT)
    ffn_norm = 1.0 + 0.02 * jax.random.normal(ks[14], (DEPTH, d), jnp.float32)
    w_gate = _normal(ks[15], (DEPTH, d, D_FF), d)
    w_up = _normal(ks[16], (DEPTH, d, D_FF), d)
    w_down = _normal(ks[17], (DEPTH, D_FF, d), D_FF)
    final_norm = 1.0 + 0.02 * jax.random.normal(ks[18], (d,), jnp.float32)
    return {'x': x, 'mix_norm_e': mix_norm_e, 'w_in_e': w_in_e, 'conv_w_e': conv_w_e,
            'conv_b_e': conv_b_e, 'ln_g_e': ln_g_e, 'ln_b_e': ln_b_e, 'w_pool_e': w_pool_e,
            'pool_scale_e': pool_scale_e, 'w_out_e': w_out_e, 'mix_norm_o': mix_norm_o,
            'w_in_o': w_in_o, 'conv_w_o': conv_w_o, 'w_out_o': w_out_o, 'ffn_norm': ffn_norm,
            'w_gate': w_gate, 'w_up': w_up, 'w_down': w_down, 'final_norm': final_norm}


def reference(x, mix_norm_e, w_in_e, conv_w_e, conv_b_e, ln_g_e, ln_b_e, w_pool_e,
              pool_scale_e, w_out_e, mix_norm_o, w_in_o, conv_w_o, w_out_o, ffn_norm,
              w_gate, w_up, w_down, final_norm):
    h = x
    for i in range(DEPTH):
        j = i // 2
        if i % 2 == 0:
            h = h + conv_pool_mixer(h, mix_norm_e[j], w_in_e[j], conv_w_e[j], conv_b_e[j],
                                    ln_g_e[j], ln_b_e[j], w_pool_e[j], pool_scale_e[j],
                                    w_out_e[j])
        else:
            h = h + short_conv_mixer(h, mix_norm_o[j], w_in_o[j], conv_w_o[j], w_out_o[j])
        h = h + swiglu(rms_norm(h, ffn_norm[i]), w_gate[i], w_up[i], w_down[i])
    return rms_norm(h, final_norm)
```

```python
import functools

import jax
import jax.numpy as jnp
from jax import lax
from jax.experimental import pallas as pl
from jax.experimental.pallas import tpu as pltpu

D_MODEL = 2048
D_CONV = D_MODEL // 2
CONV_WIDTH = 31
D_POOL = D_MODEL // 2
POOL_WINDOWS = (2, 4, 8, 16)
POOL_GROUP = D_POOL // len(POOL_WINDOWS)
D_SHORT = D_MODEL
SHORT_WIDTH = 3
D_FF = -(-8 * D_MODEL // (3 * 256)) * 256
EPS = 1e-6

CONV_HALO = 32
POOL_HALO = 16
SHORT_HALO = 8

TM_MIX_E = 512
TM_MIX_O = 512
TC_MIX_O = 512
TM_FFN = 512
TF_FFN = 512
CONV_ROWS = 64
CONV_LANES = 256
NORM_ROWS = 32
VMEM_LIMIT = 56 * 1024 * 1024

_F32 = jnp.float32
_BF16 = jnp.bfloat16


def _rms(xf, g):
    return xf * lax.rsqrt(jnp.mean(xf * xf, axis=-1, keepdims=True) + EPS) * g


def _sigmoid(x):
    return 1.0 / (1.0 + jnp.exp(-x))


def _resident(shape):
    nd = len(shape)
    return pl.BlockSpec(shape, lambda *_: (0,) * nd, pipeline_mode=pl.Buffered(1))


def _mix_e_kernel(x_ref, g_ref, win_ref, cw_ref, cb_ref, lng_ref, lnb_ref, wp_ref, ps_ref,
                  y_ref, ext_a, ext_b, c_ref, *, tm):
    s = pl.program_id(1)

    @pl.when(s == 0)
    def _():
        ext_a[0:CONV_HALO, :] = jnp.zeros((CONV_HALO, D_CONV), _F32)
        ext_b[0:POOL_HALO, :] = jnp.zeros((POOL_HALO, D_POOL), _F32)

    hn = _rms(x_ref[...], g_ref[...]).astype(_BF16)
    u = jnp.dot(hn, win_ref[...], preferred_element_type=_F32)
    ext_a[CONV_HALO:CONV_HALO + tm, :] = u[:, :D_CONV] * _sigmoid(u[:, D_CONV:2 * D_CONV])
    ext_b[POOL_HALO:POOL_HALO + tm, :] = u[:, 2 * D_CONV:]

    slab_rows = CONV_ROWS + CONV_HALO
    first_off = CONV_HALO - (CONV_WIDTH - 1)

    def conv_chunk(i, carry):
        r0 = pl.multiple_of(i * CONV_ROWS, CONV_ROWS)
        for lb in range(D_CONV // CONV_LANES):
            lanes = slice(lb * CONV_LANES, (lb + 1) * CONV_LANES)
            slab = ext_a[pl.ds(r0, slab_rows), lanes]
            acc = jnp.zeros((CONV_ROWS, CONV_LANES), _F32)
            for r in range(8):
                rot = slab if r == 0 else pltpu.roll(slab, slab_rows - r, axis=0)
                for q in range(CONV_HALO // 8 + 1):
                    k = 8 * q + r - first_off
                    if 0 <= k < CONV_WIDTH:
                        acc = acc + cw_ref[k:k + 1, lanes] * rot[8 * q:8 * q + CONV_ROWS, :]
            c_ref[pl.ds(r0, CONV_ROWS), lanes] = acc
        return carry

    lax.fori_loop(0, tm // CONV_ROWS, conv_chunk, 0)

    def norm_chunk(i, carry):
        r0 = pl.multiple_of(i * NORM_ROWS, NORM_ROWS)
        c = c_ref[pl.ds(r0, NORM_ROWS), :] + cb_ref[...]
        mu = jnp.mean(c, axis=-1, keepdims=True)
        xc = c - mu
        var = jnp.mean(xc * xc, axis=-1, keepdims=True)
        ln = xc * lax.rsqrt(var + EPS) * lng_ref[...] + lnb_ref[...]
        y_ref[pl.ds(r0, NORM_ROWS), 0:D_CONV] = (ln * _sigmoid(ln)).astype(_BF16)
        return carry

    lax.fori_loop(0, tm // NORM_ROWS, norm_chunk, 0)

    pos = s * tm + lax.broadcasted_iota(jnp.int32, (tm, 1), 0) + 1
    for gi, w in enumerate(POOL_WINDOWS):
        lo = gi * POOL_GROUP
        xg = ext_b[POOL_HALO:POOL_HALO + tm, lo:lo + POOL_GROUP]
        ssum = xg
        for j in range(1, w):
            ssum = ssum + ext_b[POOL_HALO - j:POOL_HALO - j + tm, lo:lo + POOL_GROUP]
        cnt = jnp.minimum(pos, w).astype(_F32)
        p = (ssum / cnt - xg).astype(_BF16)
        q = jnp.dot(p, wp_ref[gi], preferred_element_type=_F32) * ps_ref[:, lo:lo + POOL_GROUP]
        y_ref[:, D_CONV + lo:D_CONV + lo + POOL_GROUP] = q.astype(_BF16)

    ext_a[0:CONV_HALO, :] = ext_a[tm:tm + CONV_HALO, :]
    ext_b[0:POOL_HALO, :] = ext_b[tm:tm + POOL_HALO, :]


def _mix_e_call(x, norm_g, w_in, conv_w, conv_b, ln_g, ln_b, w_pool, pool_scale):
    b, s, d = x.shape
    tm = TM_MIX_E
    return pl.pallas_call(
        functools.partial(_mix_e_kernel, tm=tm),
        grid=(b, s // tm),
        in_specs=[
            pl.BlockSpec((None, tm, d), lambda i, j: (i, j, 0)),
            _resident((1, d)),
            _resident(w_in.shape),
            _resident(conv_w.shape),
            _resident((1, D_CONV)),
            _resident((1, D_CONV)),
            _resident((1, D_CONV)),
            _resident(w_pool.shape),
            _resident((1, D_POOL)),
        ],
        out_specs=pl.BlockSpec((None, tm, D_CONV + D_POOL), lambda i, j: (i, j, 0)),
        out_shape=jax.ShapeDtypeStruct((b, s, D_CONV + D_POOL), _BF16),
        scratch_shapes=[
            pltpu.VMEM((CONV_HALO + tm, D_CONV), _F32),
            pltpu.VMEM((POOL_HALO + tm, D_POOL), _F32),
            pltpu.VMEM((tm, D_CONV), _F32),
        ],
        compiler_params=pltpu.CompilerParams(
            dimension_semantics=("arbitrary", "arbitrary"),
            vmem_limit_bytes=VMEM_LIMIT),
        name="mix_e",
    )(x, norm_g.reshape(1, d), w_in, conv_w, conv_b.reshape(1, -1), ln_g.reshape(1, -1),
      ln_b.reshape(1, -1), w_pool, pool_scale.reshape(1, -1))


def _mix_o_kernel(x_ref, g_ref, wb_ref, wc_ref, wv_ref, cw_ref, y_ref, hn_ref, hist_ref, ext_ref,
                  *, tm):
    s = pl.program_id(1)
    j = pl.program_id(2)

    @pl.when(j == 0)
    def _():
        hn_ref[...] = _rms(x_ref[...], g_ref[...]).astype(_BF16)

    @pl.when(s == 0)
    def _():
        hist_ref[j] = jnp.zeros(hist_ref.shape[1:], _F32)

    hn = hn_ref[...]
    gate_b = jnp.dot(hn, wb_ref[...], preferred_element_type=_F32)
    gate_c = jnp.dot(hn, wc_ref[...], preferred_element_type=_F32)
    v = jnp.dot(hn, wv_ref[...], preferred_element_type=_F32)
    cv = gate_c * v
    ext_ref[0:SHORT_HALO, :] = hist_ref[j]
    ext_ref[SHORT_HALO:SHORT_HALO + tm, :] = cv
    conv = cw_ref[SHORT_WIDTH - 1:SHORT_WIDTH, :] * cv
    for k in range(SHORT_WIDTH - 1):
        off = SHORT_HALO - (SHORT_WIDTH - 1) + k
        conv = conv + cw_ref[k:k + 1, :] * ext_ref[off:off + tm, :]
    y_ref[...] = (gate_b * conv).astype(_BF16)
    hist_ref[j] = ext_ref[tm:tm + SHORT_HALO, :]


def _mix_o_call(x, norm_g, w_in, conv_w):
    b, s, d = x.shape
    tm, tc = TM_MIX_O, TC_MIX_O
    nc = D_SHORT // tc
    return pl.pallas_call(
        functools.partial(_mix_o_kernel, tm=tm),
        grid=(b, s // tm, nc),
        in_specs=[
            pl.BlockSpec((None, tm, d), lambda i, j, c: (i, j, 0)),
            _resident((1, d)),
            pl.BlockSpec((d, tc), lambda i, j, c: (0, c)),
            pl.BlockSpec((d, tc), lambda i, j, c: (0, nc + c)),
            pl.BlockSpec((d, tc), lambda i, j, c: (0, 2 * nc + c)),
            pl.BlockSpec((SHORT_WIDTH, tc), lambda i, j, c: (0, c)),
        ],
        out_specs=pl.BlockSpec((None, tm, tc), lambda i, j, c: (i, j, c)),
        out_shape=jax.ShapeDtypeStruct((b, s, D_SHORT), _BF16),
        scratch_shapes=[
            pltpu.VMEM((tm, d), _BF16),
            pltpu.VMEM((nc, SHORT_HALO, tc), _F32),
            pltpu.VMEM((SHORT_HALO + tm, tc), _F32),
        ],
        compiler_params=pltpu.CompilerParams(
            dimension_semantics=("arbitrary", "arbitrary", "arbitrary"),
            vmem_limit_bytes=VMEM_LIMIT),
        name="mix_o",
    )(x, norm_g.reshape(1, d), w_in, w_in, w_in, conv_w)


def _ffn_kernel(x_ref, y_ref, wout_ref, g_ref, wg_ref, wu_ref, wd_ref, fg_ref, o_ref, hn_ref,
                *, final_norm):
    f = pl.program_id(1)

    @pl.when(f == 0)
    def _():
        h = x_ref[...] + jnp.dot(y_ref[...], wout_ref[...], preferred_element_type=_F32)
        o_ref[...] = h
        hn_ref[...] = _rms(h, g_ref[...]).astype(_BF16)

    hn = hn_ref[...]
    gate = jnp.dot(hn, wg_ref[...], preferred_element_type=_F32)
    up = jnp.dot(hn, wu_ref[...], preferred_element_type=_F32)
    z = (gate * _sigmoid(gate) * up).astype(_BF16)
    o_ref[...] += jnp.dot(z, wd_ref[...], preferred_element_type=_F32)

    if final_norm:
        @pl.when(f == pl.num_programs(1) - 1)
        def _():
            o_ref[...] = _rms(o_ref[...], fg_ref[...])


def _ffn_call(x, y, w_out, norm_g, w_gate, w_up, w_down, final_g, *, final_norm):
    b, s, d = x.shape
    t = b * s
    tm, tf = TM_FFN, TF_FFN
    out = pl.pallas_call(
        functools.partial(_ffn_kernel, final_norm=final_norm),
        grid=(t // tm, D_FF // tf),
        in_specs=[
            pl.BlockSpec((tm, d), lambda i, f: (i, 0)),
            pl.BlockSpec((tm, d), lambda i, f: (i, 0)),
            _resident(w_out.shape),
            _resident((1, d)),
            pl.BlockSpec((d, tf), lambda i, f: (0, f)),
            pl.BlockSpec((d, tf), lambda i, f: (0, f)),
            pl.BlockSpec((tf, d), lambda i, f: (f, 0)),
            _resident((1, d)),
        ],
        out_specs=pl.BlockSpec((tm, d), lambda i, f: (i, 0)),
        out_shape=jax.ShapeDtypeStruct((t, d), _F32),
        scratch_shapes=[pltpu.VMEM((tm, d), _BF16)],
        compiler_params=pltpu.CompilerParams(
            dimension_semantics=("arbitrary", "arbitrary"),
            vmem_limit_bytes=VMEM_LIMIT),
        name="ffn_final" if final_norm else "ffn",
    )(x.reshape(t, d), y.reshape(t, d), w_out, norm_g.reshape(1, d), w_gate, w_up, w_down,
      final_g.reshape(1, d))
    return out.reshape(b, s, d)


def kernel(x, mix_norm_e, w_in_e, conv_w_e, conv_b_e, ln_g_e, ln_b_e, w_pool_e, pool_scale_e,
           w_out_e, mix_norm_o, w_in_o, conv_w_o, w_out_o, ffn_norm, w_gate, w_up, w_down,
           final_norm):
    bf = lambda w: w.astype(_BF16)
    y = _mix_e_call(x, mix_norm_e[0], bf(w_in_e[0]), conv_w_e[0], conv_b_e[0], ln_g_e[0],
                    ln_b_e[0], bf(w_pool_e[0]), pool_scale_e[0])
    h = _ffn_call(x, y, bf(w_out_e[0]), ffn_norm[0], bf(w_gate[0]), bf(w_up[0]), bf(w_down[0]),
                  final_norm, final_norm=False)
    y = _mix_o_call(h, mix_norm_o[0], bf(w_in_o[0]), conv_w_o[0])
    return _ffn_call(h, y, bf(w_out_o[0]), ffn_norm[1], bf(w_gate[1]), bf(w_up[1]),
                     bf(w_down[1]), final_norm, final_norm=True)
```

```python
import functools

import jax
import jax.numpy as jnp
from jax import lax
from jax.experimental import pallas as pl
from jax.experimental.pallas import tpu as pltpu

D_MODEL = 2048
D_CONV = D_MODEL // 2
CONV_WIDTH = 31
D_POOL = D_MODEL // 2
POOL_WINDOWS = (2, 4, 8, 16)
POOL_GROUP = D_POOL // len(POOL_WINDOWS)
D_SHORT = D_MODEL
SHORT_WIDTH = 3
D_FF = -(-8 * D_MODEL // (3 * 256)) * 256
EPS = 1e-6

SUBLANES = 8
CONV_HALO = 32
POOL_HALO = 16
SHORT_HALO = 8

TM_MIX_E = 512
TM_MIX_O = 512
TC_MIX_O = 512
TM_FFN = 1024
TF_FFN = 512
CONV_ROWS = 64
CONV_LANES = 256
NORM_ROWS = 32
VMEM_LIMIT = 58 * 1024 * 1024

_F32 = jnp.float32
_BF16 = jnp.bfloat16


def _rms(xf, g):
    return xf * lax.rsqrt(jnp.mean(xf * xf, axis=-1, keepdims=True) + EPS) * g


def _sigmoid(x):
    return 1.0 / (1.0 + jnp.exp(-x))


def _resident(shape):
    nd = len(shape)
    return pl.BlockSpec(shape, lambda *_: (0,) * nd, pipeline_mode=pl.Buffered(1))


def _mix_e_kernel(x_ref, g_ref, win_ref, cw_ref, cb_ref, lng_ref, lnb_ref, wp_ref, ps_ref,
                  wout_ref, o_ref, ext_a, ext_b, c_ref, y_ref, *, tm):
    s = pl.program_id(1)

    @pl.when(s == 0)
    def _():
        ext_a[0:CONV_HALO, :] = jnp.zeros((CONV_HALO, D_CONV), _F32)
        ext_b[0:POOL_HALO, :] = jnp.zeros((POOL_HALO, D_POOL), _F32)

    hn = _rms(x_ref[...], g_ref[...]).astype(_BF16)
    u = jnp.dot(hn, win_ref[...], preferred_element_type=_F32)
    ext_a[CONV_HALO:CONV_HALO + tm, :] = u[:, :D_CONV] * _sigmoid(u[:, D_CONV:2 * D_CONV])
    ext_b[POOL_HALO:POOL_HALO + tm, :] = u[:, 2 * D_CONV:]

    slab_rows = CONV_ROWS + CONV_HALO
    first_off = CONV_HALO - (CONV_WIDTH - 1)

    def conv_chunk(i, carry):
        r0 = pl.multiple_of(i * CONV_ROWS, CONV_ROWS)
        for lb in range(D_CONV // CONV_LANES):
            lanes = slice(lb * CONV_LANES, (lb + 1) * CONV_LANES)
            slab = ext_a[pl.ds(r0, slab_rows), lanes]
            acc = jnp.zeros((CONV_ROWS, CONV_LANES), _F32)
            for r in range(SUBLANES):
                rot = slab if r == 0 else pltpu.roll(slab, slab_rows - r, axis=0)
                for q in range(CONV_HALO // SUBLANES + 1):
                    k = SUBLANES * q + r - first_off
                    if 0 <= k < CONV_WIDTH:
                        tap = rot[SUBLANES * q:SUBLANES * q + CONV_ROWS, :]
                        acc = acc + cw_ref[k:k + 1, lanes] * tap
            c_ref[pl.ds(r0, CONV_ROWS), lanes] = acc
        return carry

    lax.fori_loop(0, tm // CONV_ROWS, conv_chunk, 0)

    def norm_chunk(i, carry):
        r0 = pl.multiple_of(i * NORM_ROWS, NORM_ROWS)
        c = c_ref[pl.ds(r0, NORM_ROWS), :] + cb_ref[...]
        mu = jnp.mean(c, axis=-1, keepdims=True)
        xc = c - mu
        var = jnp.mean(xc * xc, axis=-1, keepdims=True)
        ln = xc * lax.rsqrt(var + EPS) * lng_ref[...] + lnb_ref[...]
        y_ref[pl.ds(r0, NORM_ROWS), 0:D_CONV] = (ln * _sigmoid(ln)).astype(_BF16)
        return carry

    lax.fori_loop(0, tm // NORM_ROWS, norm_chunk, 0, unroll=2)

    pos = s * tm + lax.broadcasted_iota(jnp.int32, (tm, 1), 0) + 1
    for gi, w in enumerate(POOL_WINDOWS):
        lo = gi * POOL_GROUP
        xg = ext_b[POOL_HALO:POOL_HALO + tm, lo:lo + POOL_GROUP]
        ssum = xg
        for j in range(1, w):
            ssum = ssum + ext_b[POOL_HALO - j:POOL_HALO - j + tm, lo:lo + POOL_GROUP]
        cnt = jnp.minimum(pos, w).astype(_F32)
        p = (ssum / cnt - xg).astype(_BF16)
        q = jnp.dot(p, wp_ref[gi], preferred_element_type=_F32) * ps_ref[:, lo:lo + POOL_GROUP]
        y_ref[:, D_CONV + lo:D_CONV + lo + POOL_GROUP] = q.astype(_BF16)

    o_ref[...] = x_ref[...] + jnp.dot(y_ref[...], wout_ref[...], preferred_element_type=_F32)

    ext_a[0:CONV_HALO, :] = ext_a[tm:tm + CONV_HALO, :]
    ext_b[0:POOL_HALO, :] = ext_b[tm:tm + POOL_HALO, :]


def _mix_e_call(x, norm_g, w_in, conv_w, conv_b, ln_g, ln_b, w_pool, pool_scale, w_out):
    b, s, d = x.shape
    tm = TM_MIX_E
    return pl.pallas_call(
        functools.partial(_mix_e_kernel, tm=tm),
        grid=(b, s // tm),
        in_specs=[
            pl.BlockSpec((None, tm, d), lambda i, j: (i, j, 0)),
            _resident((1, d)),
            _resident(w_in.shape),
            _resident(conv_w.shape),
            _resident((1, D_CONV)),
            _resident((1, D_CONV)),
            _resident((1, D_CONV)),
            _resident(w_pool.shape),
            _resident((1, D_POOL)),
            _resident(w_out.shape),
        ],
        out_specs=pl.BlockSpec((None, tm, d), lambda i, j: (i, j, 0)),
        out_shape=jax.ShapeDtypeStruct((b, s, d), _F32),
        scratch_shapes=[
            pltpu.VMEM((CONV_HALO + tm, D_CONV), _F32),
            pltpu.VMEM((POOL_HALO + tm, D_POOL), _F32),
            pltpu.VMEM((tm, D_CONV), _F32),
            pltpu.VMEM((tm, D_CONV + D_POOL), _BF16),
        ],
        compiler_params=pltpu.CompilerParams(
            dimension_semantics=("arbitrary", "arbitrary"),
            vmem_limit_bytes=VMEM_LIMIT),
        name="mix_e",
    )(x, norm_g.reshape(1, d), w_in, conv_w, conv_b.reshape(1, -1), ln_g.reshape(1, -1),
      ln_b.reshape(1, -1), w_pool, pool_scale.reshape(1, -1), w_out)


def _mix_o_kernel(x_ref, g_ref, wb_ref, wc_ref, wv_ref, cw_ref, wout_ref, o_ref,
                  hn_ref, hist_ref, ext_ref, *, tm):
    s = pl.program_id(1)
    j = pl.program_id(2)

    @pl.when(j == 0)
    def _():
        x = x_ref[...]
        o_ref[...] = x
        hn_ref[...] = _rms(x, g_ref[...]).astype(_BF16)

    @pl.when(s == 0)
    def _():
        hist_ref[j] = jnp.zeros(hist_ref.shape[1:], _F32)

    hn = hn_ref[...]
    gate_b = jnp.dot(hn, wb_ref[...], preferred_element_type=_F32)
    gate_c = jnp.dot(hn, wc_ref[...], preferred_element_type=_F32)
    v = jnp.dot(hn, wv_ref[...], preferred_element_type=_F32)
    cv = gate_c * v
    ext_ref[0:SHORT_HALO, :] = hist_ref[j]
    ext_ref[SHORT_HALO:SHORT_HALO + tm, :] = cv
    conv = cw_ref[SHORT_WIDTH - 1:SHORT_WIDTH, :] * cv
    for k in range(SHORT_WIDTH - 1):
        off = SHORT_HALO - (SHORT_WIDTH - 1) + k
        conv = conv + cw_ref[k:k + 1, :] * ext_ref[off:off + tm, :]
    hist_ref[j] = ext_ref[tm:tm + SHORT_HALO, :]
    y = (gate_b * conv).astype(_BF16)
    o_ref[...] += jnp.dot(y, wout_ref[...], preferred_element_type=_F32)


def _mix_o_call(x, norm_g, w_in, conv_w, w_out):
    b, s, d = x.shape
    tm, tc = TM_MIX_O, TC_MIX_O
    nc = D_SHORT // tc
    return pl.pallas_call(
        functools.partial(_mix_o_kernel, tm=tm),
        grid=(b, s // tm, nc),
        in_specs=[
            pl.BlockSpec((None, tm, d), lambda i, j, c: (i, j, 0)),
            _resident((1, d)),
            pl.BlockSpec((d, tc), lambda i, j, c: (0, c)),
            pl.BlockSpec((d, tc), lambda i, j, c: (0, nc + c)),
            pl.BlockSpec((d, tc), lambda i, j, c: (0, 2 * nc + c)),
            pl.BlockSpec((SHORT_WIDTH, tc), lambda i, j, c: (0, c)),
            pl.BlockSpec((tc, d), lambda i, j, c: (c, 0)),
        ],
        out_specs=pl.BlockSpec((None, tm, d), lambda i, j, c: (i, j, 0)),
        out_shape=jax.ShapeDtypeStruct((b, s, d), _F32),
        scratch_shapes=[
            pltpu.VMEM((tm, d), _BF16),
            pltpu.VMEM((nc, SHORT_HALO, tc), _F32),
            pltpu.VMEM((SHORT_HALO + tm, tc), _F32),
        ],
        compiler_params=pltpu.CompilerParams(
            dimension_semantics=("arbitrary", "arbitrary", "arbitrary"),
            vmem_limit_bytes=VMEM_LIMIT),
        name="mix_o",
    )(x, norm_g.reshape(1, d), w_in, w_in, w_in, conv_w, w_out)


def _ffn_kernel(x_ref, g_ref, wg_ref, wu_ref, wd_ref, fg_ref, o_ref, hn_ref, *, final_norm):
    f = pl.program_id(1)

    @pl.when(f == 0)
    def _():
        h = x_ref[...]
        o_ref[...] = h
        hn_ref[...] = _rms(h, g_ref[...]).astype(_BF16)

    hn = hn_ref[...]
    gate = jnp.dot(hn, wg_ref[...], preferred_element_type=_F32)
    up = jnp.dot(hn, wu_ref[...], preferred_element_type=_F32)
    z = (gate * _sigmoid(gate) * up).astype(_BF16)
    o_ref[...] += jnp.dot(z, wd_ref[...], preferred_element_type=_F32)

    if final_norm:
        @pl.when(f == pl.num_programs(1) - 1)
        def _():
            o_ref[...] = _rms(o_ref[...], fg_ref[...])


def _ffn_call(x, norm_g, w_gate, w_up, w_down, final_g, *, final_norm):
    b, s, d = x.shape
    t = b * s
    tm, tf = TM_FFN, TF_FFN
    out = pl.pallas_call(
        functools.partial(_ffn_kernel, final_norm=final_norm),
        grid=(t // tm, D_FF // tf),
        in_specs=[
            pl.BlockSpec((tm, d), lambda i, f: (i, 0)),
            _resident((1, d)),
            pl.BlockSpec((d, tf), lambda i, f: (0, f)),
            pl.BlockSpec((d, tf), lambda i, f: (0, f)),
            pl.BlockSpec((tf, d), lambda i, f: (f, 0)),
            _resident((1, d)),
        ],
        out_specs=pl.BlockSpec((tm, d), lambda i, f: (i, 0)),
        out_shape=jax.ShapeDtypeStruct((t, d), _F32),
        scratch_shapes=[pltpu.VMEM((tm, d), _BF16)],
        compiler_params=pltpu.CompilerParams(
            dimension_semantics=("arbitrary", "arbitrary"),
            vmem_limit_bytes=VMEM_LIMIT),
        name="ffn_final" if final_norm else "ffn",
    )(x.reshape(t, d), norm_g.reshape(1, d), w_gate, w_up, w_down, final_g.reshape(1, d))
    return out.reshape(b, s, d)


def kernel(x, mix_norm_e, w_in_e, conv_w_e, conv_b_e, ln_g_e, ln_b_e, w_pool_e, pool_scale_e,
           w_out_e, mix_norm_o, w_in_o, conv_w_o, w_out_o, ffn_norm, w_gate, w_up, w_down,
           final_norm):
    bf = lambda w: w.astype(_BF16)
    h = _mix_e_call(x, mix_norm_e[0], bf(w_in_e[0]), conv_w_e[0], conv_b_e[0], ln_g_e[0],
                    ln_b_e[0], bf(w_pool_e[0]), pool_scale_e[0], bf(w_out_e[0]))
    h = _ffn_call(h, ffn_norm[0], bf(w_gate[0]), bf(w_up[0]), bf(w_down[0]), final_norm,
                  final_norm=False)
    h = _mix_o_call(h, mix_norm_o[0], bf(w_in_o[0]), conv_w_o[0], bf(w_out_o[0]))
    return _ffn_call(h, ffn_norm[1], bf(w_gate[1]), bf(w_up[1]), bf(w_down[1]), final_norm,
                     final_norm=True)
```

```python
import functools
from typing import NamedTuple

import jax
import jax.numpy as jnp
from jax import lax
from jax.experimental import pallas as pl
from jax.experimental.pallas import tpu as pltpu

D_MODEL = 2048
D_CONV = D_MODEL // 2
CONV_WIDTH = 31
D_POOL = D_MODEL // 2
POOL_WINDOWS = (2, 4, 8, 16)
POOL_GROUP = D_POOL // len(POOL_WINDOWS)
D_SHORT = D_MODEL
SHORT_WIDTH = 3
D_FF = -(-8 * D_MODEL // (3 * 256)) * 256
EPS = 1e-6

SUBLANES = 8
LANES = 128
BF16_ROWS = 16
CONV_HALO = 32
POOL_HALO = 16
SHORT_HALO = 8

TM_MIX_E = 256
MIX_E_BLOCKS = 4
TM_MIX_O = 512
TC_MIX_O = 512
TM_FFN = 1024
TF_FFN = 512
CONV_ROWS = TM_MIX_E // MIX_E_BLOCKS
CONV_LANES = D_CONV // MIX_E_BLOCKS
CONV_ACC_LANES = 128
NORM_ROWS = 32
VMEM_LIMIT = 58 * 1024 * 1024

_F32 = jnp.float32
_BF16 = jnp.bfloat16


def _rms(xf, g):
    return xf * lax.rsqrt(jnp.mean(xf * xf, axis=-1, keepdims=True) + EPS) * g


def _sigmoid(x):
    return 1.0 / (1.0 + jnp.exp(-x))


def _resident(shape):
    nd = len(shape)
    return pl.BlockSpec(shape, lambda *_: (0,) * nd, pipeline_mode=pl.Buffered(1))


class _CastBlocks(NamedTuple):
    in_spec: pl.BlockSpec
    out_spec: pl.BlockSpec
    out_shape: jax.ShapeDtypeStruct


def _cast_blocks(w, layer, n_steps, step_of):
    _, rows, cols = w.shape
    blk = next(r for r in range(BF16_ROWS, rows + 1, BF16_ROWS)
               if rows % r == 0 and rows // r <= n_steps)
    last = rows // blk - 1
    return _CastBlocks(
        pl.BlockSpec((None, blk, cols), lambda *g: (layer, jnp.minimum(step_of(*g), last), 0)),
        pl.BlockSpec((blk, cols), lambda *g: (jnp.minimum(step_of(*g), last), 0)),
        jax.ShapeDtypeStruct((rows, cols), _BF16))


def _cast_side_job(cast_in, cast_out):
    for src, dst in zip(cast_in, cast_out):
        dst[...] = src[...].astype(_BF16)


def _mix_e_step(x_ref, xo_ref, g_ref, win_ref, cw_ref, cb_ref, lng_ref, lnb_ref, wp_ref,
                ps_ref, wout_ref, o_ref, hn_ref, oacc_ref, a_new, a_old, b_new, b_old,
                y_new, y_old, *, tm, tiles_per_seq):
    k = pl.program_id(0)

    keep = (k % tiles_per_seq != 0).astype(_F32)
    a_new[:, 0:CONV_HALO, :] = a_old[:, tm:tm + CONV_HALO, :] * keep
    b_new[:, 0:POOL_HALO, :] = b_old[:, tm:tm + POOL_HALO, :] * keep

    hn_ref[...] = _rms(x_ref[...], g_ref[...]).astype(_BF16)

    seq_tile = (k + tiles_per_seq - 1) % tiles_per_seq
    pos = seq_tile * tm + lax.broadcasted_iota(jnp.int32, (tm, 1), 0) + 1
    for gi, w in enumerate(POOL_WINDOWS):
        xg = b_old[gi, POOL_HALO:POOL_HALO + tm, :]
        ssum = xg
        for j in range(1, w):
            ssum = ssum + b_old[gi, POOL_HALO - j:POOL_HALO - j + tm, :]
        cnt = jnp.minimum(pos, w).astype(_F32)
        p = (ssum / cnt - xg).astype(_BF16)
        lanes = slice(gi * POOL_GROUP, (gi + 1) * POOL_GROUP)
        q = jnp.dot(p, wp_ref[gi], preferred_element_type=_F32) * ps_ref[:, lanes]
        y_new[:, D_CONV + gi * POOL_GROUP:D_CONV + (gi + 1) * POOL_GROUP] = q.astype(_BF16)

    slab_rows = CONV_ROWS + CONV_HALO
    first_off = CONV_HALO - (CONV_WIDTH - 1)

    def block(n, carry):
        u = jnp.dot(hn_ref[...], win_ref[n], preferred_element_type=_F32)
        for h in range(CONV_LANES // LANES):
            val = u[:, 2 * h * LANES:(2 * h + 1) * LANES]
            gate = u[:, (2 * h + 1) * LANES:(2 * h + 2) * LANES]
            a_new[n, CONV_HALO:CONV_HALO + tm, h * LANES:(h + 1) * LANES] = val * _sigmoid(gate)
        b_new[n, POOL_HALO:POOL_HALO + tm, :] = u[:, 2 * CONV_LANES:]

        oacc_ref[n] = jnp.dot(y_old[...], wout_ref[n], preferred_element_type=_F32)

        r0 = pl.multiple_of(n * CONV_ROWS, CONV_ROWS)
        pieces = []
        for lb in range(D_CONV // CONV_ACC_LANES):
            cl = slice(lb * CONV_ACC_LANES, (lb + 1) * CONV_ACC_LANES)
            blk, sub = divmod(lb * CONV_ACC_LANES, CONV_LANES)
            slab = a_old[blk, pl.ds(r0, slab_rows), sub:sub + CONV_ACC_LANES]
            acc = jnp.zeros((CONV_ROWS, CONV_ACC_LANES), _F32)
            for r in range(SUBLANES):
                rot = slab if r == 0 else pltpu.roll(slab, slab_rows - r, axis=0)
                for q in range(CONV_HALO // SUBLANES + 1):
                    j = SUBLANES * q + r - first_off
                    if 0 <= j < CONV_WIDTH:
                        tap = rot[SUBLANES * q:SUBLANES * q + CONV_ROWS, :]
                        acc = acc + cw_ref[j:j + 1, cl] * tap
            pieces.append(acc)
        for h in range(CONV_ROWS // NORM_ROWS):
            rows = slice(h * NORM_ROWS, (h + 1) * NORM_ROWS)
            c = jnp.concatenate([p[rows] for p in pieces], axis=1) + cb_ref[...]
            mu = jnp.mean(c, axis=-1, keepdims=True)
            xc = c - mu
            var = jnp.mean(xc * xc, axis=-1, keepdims=True)
            ln = xc * lax.rsqrt(var + EPS) * lng_ref[...] + lnb_ref[...]
            y_new[pl.ds(r0 + h * NORM_ROWS, NORM_ROWS), 0:D_CONV] = (
                ln * _sigmoid(ln)).astype(_BF16)
        return carry

    lax.fori_loop(0, MIX_E_BLOCKS, block, 0)

    out_cols = D_MODEL // MIX_E_BLOCKS
    for n in range(MIX_E_BLOCKS):
        oc = slice(n * out_cols, (n + 1) * out_cols)
        o_ref[:, oc] = xo_ref[:, oc] + oacc_ref[n]


def _mix_e_kernel(*refs, n_cast, **kw):
    (x_ref, xo_ref, g_ref, win_ref, cw_ref, cb_ref, lng_ref, lnb_ref, wp_ref, ps_ref,
     wout_ref) = refs[:11]
    cast_in = refs[11:11 + n_cast]
    o_ref = refs[11 + n_cast]
    cast_out = refs[12 + n_cast:12 + 2 * n_cast]
    hn_ref, oacc_ref, a0, a1, b0, b1, y0, y1 = refs[12 + 2 * n_cast:]
    k = pl.program_id(0)
    shared = (x_ref, xo_ref, g_ref, win_ref, cw_ref, cb_ref, lng_ref, lnb_ref, wp_ref, ps_ref,
              wout_ref, o_ref, hn_ref, oacc_ref)

    _cast_side_job(cast_in, cast_out)

    @pl.when(k == 0)
    def _():
        for ref in (a0, a1, b0, b1, y0, y1):
            ref[...] = jnp.zeros(ref.shape, ref.dtype)

    @pl.when(k % 2 == 0)
    def _():
        _mix_e_step(*shared, a0, a1, b0, b1, y0, y1, **kw)

    @pl.when(k % 2 == 1)
    def _():
        _mix_e_step(*shared, a1, a0, b1, b0, y1, y0, **kw)


def _mix_e_call(x, norm_g, w_in, conv_w, conv_b, ln_g, ln_b, w_pool, pool_scale, w_out,
                cast_weights):
    b, s, d = x.shape
    tm, nb = TM_MIX_E, MIX_E_BLOCKS
    n_tiles = b * s // tm
    casts = [_cast_blocks(w, layer, n_tiles + 2, lambda k: k) for w, layer in cast_weights]
    w3 = w_in.reshape(d, 3, nb, CONV_LANES // LANES, LANES)
    vg = jnp.stack([w3[:, 0], w3[:, 1]], axis=3).reshape(d, nb, 2 * CONV_LANES)
    w_in = jnp.concatenate([vg, w3[:, 2].reshape(d, nb, CONV_LANES)], axis=2).transpose(1, 0, 2)
    w_out = w_out.reshape(d, nb, d // nb).transpose(1, 0, 2)
    x2 = x.reshape(b * s, d)
    out, *cast = pl.pallas_call(
        functools.partial(_mix_e_kernel, n_cast=len(casts), tm=tm, tiles_per_seq=s // tm),
        grid=(n_tiles + 2,),
        in_specs=[
            pl.BlockSpec((tm, d), lambda k: (jnp.minimum(k, n_tiles - 1), 0)),
            pl.BlockSpec((tm, d), lambda k: (jnp.maximum(k - 2, 0), 0)),
            _resident((1, d)),
            _resident(w_in.shape),
            _resident(conv_w.shape),
            _resident((1, D_CONV)),
            _resident((1, D_CONV)),
            _resident((1, D_CONV)),
            _resident(w_pool.shape),
            _resident((1, D_POOL)),
            _resident(w_out.shape),
        ] + [c.in_spec for c in casts],
        out_specs=[pl.BlockSpec((tm, d), lambda k: (jnp.maximum(k - 2, 0), 0))]
        + [c.out_spec for c in casts],
        out_shape=[jax.ShapeDtypeStruct((b * s, d), _F32)] + [c.out_shape for c in casts],
        scratch_shapes=[
            pltpu.VMEM((tm, d), _BF16),
            pltpu.VMEM((nb, tm, d // nb), _F32),
            pltpu.VMEM((nb, CONV_HALO + tm, CONV_LANES), _F32),
            pltpu.VMEM((nb, CONV_HALO + tm, CONV_LANES), _F32),
            pltpu.VMEM((nb, POOL_HALO + tm, POOL_GROUP), _F32),
            pltpu.VMEM((nb, POOL_HALO + tm, POOL_GROUP), _F32),
            pltpu.VMEM((tm, D_CONV + D_POOL), _BF16),
            pltpu.VMEM((tm, D_CONV + D_POOL), _BF16),
        ],
        compiler_params=pltpu.CompilerParams(
            dimension_semantics=("arbitrary",),
            vmem_limit_bytes=VMEM_LIMIT),
        name="mix_e",
    )(x2, x2, norm_g.reshape(1, d), w_in, conv_w, conv_b.reshape(1, -1),
      ln_g.reshape(1, -1), ln_b.reshape(1, -1), w_pool, pool_scale.reshape(1, -1), w_out,
      *[w for w, _ in cast_weights])
    return out.reshape(b, s, d), cast


def _mix_o_kernel(*refs, n_cast, tm):
    x_ref, g_ref, wb_ref, wc_ref, wv_ref, cw_ref, wout_ref = refs[:7]
    cast_in = refs[7:7 + n_cast]
    o_ref = refs[7 + n_cast]
    cast_out = refs[8 + n_cast:8 + 2 * n_cast]
    hn_ref, hist_ref, ext_ref = refs[8 + 2 * n_cast:]
    s = pl.program_id(1)
    j = pl.program_id(2)

    @pl.when(j == 0)
    def _():
        x = x_ref[...]
        o_ref[...] = x
        hn_ref[...] = _rms(x, g_ref[...]).astype(_BF16)

    @pl.when(s == 0)
    def _():
        hist_ref[j] = jnp.zeros(hist_ref.shape[1:], _F32)

    _cast_side_job(cast_in, cast_out)
    hn = hn_ref[...]
    gate_b = jnp.dot(hn, wb_ref[...], preferred_element_type=_F32)
    gate_c = jnp.dot(hn, wc_ref[...], preferred_element_type=_F32)
    v = jnp.dot(hn, wv_ref[...], preferred_element_type=_F32)
    cv = gate_c * v
    ext_ref[0:SHORT_HALO, :] = hist_ref[j]
    ext_ref[SHORT_HALO:SHORT_HALO + tm, :] = cv
    conv = cw_ref[SHORT_WIDTH - 1:SHORT_WIDTH, :] * cv
    for k in range(SHORT_WIDTH - 1):
        off = SHORT_HALO - (SHORT_WIDTH - 1) + k
        conv = conv + cw_ref[k:k + 1, :] * ext_ref[off:off + tm, :]
    hist_ref[j] = ext_ref[tm:tm + SHORT_HALO, :]
    y = (gate_b * conv).astype(_BF16)
    o_ref[...] += jnp.dot(y, wout_ref[...], preferred_element_type=_F32)


def _mix_o_call(x, norm_g, w_in, conv_w, w_out, cast_weights):
    b, s, d = x.shape
    tm, tc = TM_MIX_O, TC_MIX_O
    nc = D_SHORT // tc
    ns = s // tm
    casts = [_cast_blocks(w, layer, b * ns * nc, lambda i, j, c: (i * ns + j) * nc + c)
             for w, layer in cast_weights]
    out, *cast = pl.pallas_call(
        functools.partial(_mix_o_kernel, n_cast=len(casts), tm=tm),
        grid=(b, ns, nc),
        in_specs=[
            pl.BlockSpec((None, tm, d), lambda i, j, c: (i, j, 0)),
            _resident((1, d)),
            pl.BlockSpec((d, tc), lambda i, j, c: (0, c)),
            pl.BlockSpec((d, tc), lambda i, j, c: (0, nc + c)),
            pl.BlockSpec((d, tc), lambda i, j, c: (0, 2 * nc + c)),
            pl.BlockSpec((SHORT_WIDTH, tc), lambda i, j, c: (0, c)),
            pl.BlockSpec((tc, d), lambda i, j, c: (c, 0)),
        ] + [c.in_spec for c in casts],
        out_specs=[pl.BlockSpec((None, tm, d), lambda i, j, c: (i, j, 0))]
        + [c.out_spec for c in casts],
        out_shape=[jax.ShapeDtypeStruct((b, s, d), _F32)] + [c.out_shape for c in casts],
        scratch_shapes=[
            pltpu.VMEM((tm, d), _BF16),
            pltpu.VMEM((nc, SHORT_HALO, tc), _F32),
            pltpu.VMEM((SHORT_HALO + tm, tc), _F32),
        ],
        compiler_params=pltpu.CompilerParams(
            dimension_semantics=("arbitrary", "arbitrary", "arbitrary"),
            vmem_limit_bytes=VMEM_LIMIT),
        name="mix_o",
    )(x, norm_g.reshape(1, d), w_in, w_in, w_in, conv_w, w_out, *[w for w, _ in cast_weights])
    return out, cast


def _ffn_kernel(*refs, n_cast, final_norm):
    x_ref, g_ref, wg_ref, wu_ref, wd_ref, fg_ref = refs[:6]
    cast_in = refs[6:6 + n_cast]
    o_ref = refs[6 + n_cast]
    cast_out = refs[7 + n_cast:7 + 2 * n_cast]
    (hn_ref,) = refs[7 + 2 * n_cast:]
    f = pl.program_id(1)

    @pl.when(f == 0)
    def _():
        h = x_ref[...]
        o_ref[...] = h
        hn_ref[...] = _rms(h, g_ref[...]).astype(_BF16)

    _cast_side_job(cast_in, cast_out)
    hn = hn_ref[...]
    gate = jnp.dot(hn, wg_ref[...], preferred_element_type=_F32)
    up = jnp.dot(hn, wu_ref[...], preferred_element_type=_F32)
    z = (gate * _sigmoid(gate) * up).astype(_BF16)
    o_ref[...] += jnp.dot(z, wd_ref[...], preferred_element_type=_F32)

    if final_norm:
        @pl.when(f == pl.num_programs(1) - 1)
        def _():
            o_ref[...] = _rms(o_ref[...], fg_ref[...])


def _ffn_call(x, norm_g, w_gate, w_up, w_down, final_g, cast_weights, *, final_norm):
    b, s, d = x.shape
    t = b * s
    tm, tf = TM_FFN, TF_FFN
    nf = D_FF // tf
    casts = [_cast_blocks(w, layer, (t // tm) * nf, lambda i, f: i * nf + f)
             for w, layer in cast_weights]
    out, *cast = pl.pallas_call(
        functools.partial(_ffn_kernel, n_cast=len(casts), final_norm=final_norm),
        grid=(t // tm, nf),
        in_specs=[
            pl.BlockSpec((tm, d), lambda i, f: (i, 0)),
            _resident((1, d)),
            pl.BlockSpec((d, tf), lambda i, f: (0, f)),
            pl.BlockSpec((d, tf), lambda i, f: (0, f)),
            pl.BlockSpec((tf, d), lambda i, f: (f, 0)),
            _resident((1, d)),
        ] + [c.in_spec for c in casts],
        out_specs=[pl.BlockSpec((tm, d), lambda i, f: (i, 0))] + [c.out_spec for c in casts],
        out_shape=[jax.ShapeDtypeStruct((t, d), _F32)] + [c.out_shape for c in casts],
        scratch_shapes=[pltpu.VMEM((tm, d), _BF16)],
        compiler_params=pltpu.CompilerParams(
            dimension_semantics=("arbitrary", "arbitrary"),
            vmem_limit_bytes=VMEM_LIMIT),
        name="ffn_final" if final_norm else "ffn",
    )(x.reshape(t, d), norm_g.reshape(1, d), w_gate, w_up, w_down, final_g.reshape(1, d),
      *[w for w, _ in cast_weights])
    return out.reshape(b, s, d), cast


def kernel(x, mix_norm_e, w_in_e, conv_w_e, conv_b_e, ln_g_e, ln_b_e, w_pool_e, pool_scale_e,
           w_out_e, mix_norm_o, w_in_o, conv_w_o, w_out_o, ffn_norm, w_gate, w_up, w_down,
           final_norm):
    bf = lambda w: w.astype(_BF16)
    h, (wg0, wu0, wd0, wi1, wo1) = _mix_e_call(
        x, mix_norm_e[0], bf(w_in_e[0]), conv_w_e[0], conv_b_e[0], ln_g_e[0], ln_b_e[0],
        bf(w_pool_e[0]), pool_scale_e[0], bf(w_out_e[0]),
        cast_weights=[(w_gate, 0), (w_up, 0), (w_down, 0), (w_in_o, 0), (w_out_o, 0)])
    h, _ = _ffn_call(h, ffn_norm[0], wg0, wu0, wd0, final_norm, cast_weights=[],
                     final_norm=False)
    h, (wg1, wu1, wd1) = _mix_o_call(h, mix_norm_o[0], wi1, conv_w_o[0], wo1,
                                     cast_weights=[(w_gate, 1), (w_up, 1), (w_down, 1)])
    out, _ = _ffn_call(h, ffn_norm[1], wg1, wu1, wd1, final_norm, cast_weights=[],
                       final_norm=True)
    return out
```

```python
import functools
from typing import NamedTuple

import jax
import jax.numpy as jnp
from jax import lax
from jax.experimental import pallas as pl
from jax.experimental.pallas import tpu as pltpu

D_MODEL = 2048
D_CONV = D_MODEL // 2
CONV_WIDTH = 31
D_POOL = D_MODEL // 2
POOL_WINDOWS = (2, 4, 8, 16)
POOL_GROUP = D_POOL // len(POOL_WINDOWS)
D_SHORT = D_MODEL
SHORT_WIDTH = 3
D_FF = -(-8 * D_MODEL // (3 * 256)) * 256
EPS = 1e-6

SUBLANES = 8
LANES = 128
BF16_ROWS = 16
CONV_HALO = 32
POOL_HALO = 16
SHORT_HALO = 8

TM_MIX_E = 256
TM_MIX_O = 512
TC_MIX_O = 512
TM_FFN = 1024
TF_FFN = 512
CONV_ROWS = 64
CONV_LANES = POOL_GROUP
NORM_ROWS = 32
VMEM_LIMIT = 58 * 1024 * 1024

_F32 = jnp.float32
_BF16 = jnp.bfloat16


def _rms(xf, g):
    return xf * lax.rsqrt(jnp.mean(xf * xf, axis=-1, keepdims=True) + EPS) * g


def _sigmoid(x):
    return 1.0 / (1.0 + jnp.exp(-x))


def _resident(shape):
    nd = len(shape)
    return pl.BlockSpec(shape, lambda *_: (0,) * nd, pipeline_mode=pl.Buffered(1))


class _CastBlocks(NamedTuple):
    in_spec: pl.BlockSpec
    out_spec: pl.BlockSpec
    out_shape: jax.ShapeDtypeStruct


def _cast_blocks(w, layer, n_steps, step_of):
    _, rows, cols = w.shape
    blk = next(r for r in range(BF16_ROWS, rows + 1, BF16_ROWS)
               if rows % r == 0 and rows // r <= n_steps)
    last = rows // blk - 1
    return _CastBlocks(
        pl.BlockSpec((None, blk, cols), lambda *g: (layer, jnp.minimum(step_of(*g), last), 0)),
        pl.BlockSpec((blk, cols), lambda *g: (jnp.minimum(step_of(*g), last), 0)),
        jax.ShapeDtypeStruct((rows, cols), _BF16))


def _cast_side_job(cast_in, cast_out):
    for src, dst in zip(cast_in, cast_out):
        dst[...] = src[...].astype(_BF16)


def _mix_e_kernel(*refs, n_cast, tm):
    (x_ref, g_ref, win_ref, cw_ref, cb_ref, lng_ref, lnb_ref, wp_ref, ps_ref,
     wout_ref) = refs[:10]
    cast_in = refs[10:10 + n_cast]
    o_ref = refs[10 + n_cast]
    cast_out = refs[11 + n_cast:11 + 2 * n_cast]
    ext_a, ext_b, c_ref, y_ref = refs[11 + 2 * n_cast:]
    s = pl.program_id(1)

    @pl.when(s == 0)
    def _():
        ext_a[0:CONV_HALO, :] = jnp.zeros((CONV_HALO, D_CONV), _F32)
        ext_b[0:POOL_HALO, :] = jnp.zeros((POOL_HALO, D_POOL), _F32)

    _cast_side_job(cast_in, cast_out)
    hn = _rms(x_ref[...], g_ref[...]).astype(_BF16)
    pos = s * tm + lax.broadcasted_iota(jnp.int32, (tm, 1), 0) + 1

    slab_rows = CONV_ROWS + CONV_HALO
    first_off = CONV_HALO - (CONV_WIDTH - 1)
    blk_cols = 3 * CONV_LANES

    for n in range(D_CONV // CONV_LANES):
        lanes = slice(n * CONV_LANES, (n + 1) * CONV_LANES)
        u = jnp.dot(hn, win_ref[:, n * blk_cols:(n + 1) * blk_cols],
                    preferred_element_type=_F32)
        for h in range(CONV_LANES // LANES):
            val = u[:, 2 * h * LANES:(2 * h + 1) * LANES]
            gate = u[:, (2 * h + 1) * LANES:(2 * h + 2) * LANES]
            cl = slice(n * CONV_LANES + h * LANES, n * CONV_LANES + (h + 1) * LANES)
            ext_a[CONV_HALO:CONV_HALO + tm, cl] = val * _sigmoid(gate)
        ext_b[POOL_HALO:POOL_HALO + tm, lanes] = u[:, 2 * CONV_LANES:]

        for h in range(CONV_LANES // LANES):
            cl = slice(n * CONV_LANES + h * LANES, n * CONV_LANES + (h + 1) * LANES)
            for r0 in range(0, tm, CONV_ROWS):
                slab = ext_a[r0:r0 + slab_rows, cl]
                acc = jnp.zeros((CONV_ROWS, LANES), _F32)
                for r in range(SUBLANES):
                    rot = slab if r == 0 else pltpu.roll(slab, slab_rows - r, axis=0)
                    for q in range(CONV_HALO // SUBLANES + 1):
                        j = SUBLANES * q + r - first_off
                        if 0 <= j < CONV_WIDTH:
                            tap = rot[SUBLANES * q:SUBLANES * q + CONV_ROWS, :]
                            acc = acc + cw_ref[j:j + 1, cl] * tap
                c_ref[r0:r0 + CONV_ROWS, cl] = acc

        w = POOL_WINDOWS[n]
        xg = ext_b[POOL_HALO:POOL_HALO + tm, lanes]
        ssum = xg
        for j in range(1, w):
            ssum = ssum + ext_b[POOL_HALO - j:POOL_HALO - j + tm, lanes]
        cnt = jnp.minimum(pos, w).astype(_F32)
        p = (ssum / cnt - xg).astype(_BF16)
        q = jnp.dot(p, wp_ref[n], preferred_element_type=_F32) * ps_ref[:, lanes]
        y_ref[:, D_CONV + n * POOL_GROUP:D_CONV + (n + 1) * POOL_GROUP] = q.astype(_BF16)

    o_ref[...] = x_ref[...] + jnp.dot(y_ref[:, D_CONV:], wout_ref[D_CONV:, :],
                                      preferred_element_type=_F32)

    for r0 in range(0, tm, NORM_ROWS):
        c = c_ref[r0:r0 + NORM_ROWS, :] + cb_ref[...]
        mu = jnp.mean(c, axis=-1, keepdims=True)
        xc = c - mu
        var = jnp.mean(xc * xc, axis=-1, keepdims=True)
        ln = xc * lax.rsqrt(var + EPS) * lng_ref[...] + lnb_ref[...]
        y_ref[r0:r0 + NORM_ROWS, 0:D_CONV] = (ln * _sigmoid(ln)).astype(_BF16)

    o_ref[...] += jnp.dot(y_ref[:, 0:D_CONV], wout_ref[0:D_CONV, :],
                          preferred_element_type=_F32)

    ext_a[0:CONV_HALO, :] = ext_a[tm:tm + CONV_HALO, :]
    ext_b[0:POOL_HALO, :] = ext_b[tm:tm + POOL_HALO, :]


def _mix_e_call(x, norm_g, w_in, conv_w, conv_b, ln_g, ln_b, w_pool, pool_scale, w_out,
                cast_weights):
    b, s, d = x.shape
    tm = TM_MIX_E
    ns = s // tm
    nb = D_CONV // CONV_LANES
    casts = [_cast_blocks(w, layer, b * ns, lambda i, j: i * ns + j)
             for w, layer in cast_weights]
    w3 = w_in.reshape(d, 3, nb, CONV_LANES // LANES, LANES)
    vg = jnp.stack([w3[:, 0], w3[:, 1]], axis=3).reshape(d, nb, 2 * CONV_LANES)
    w_in = jnp.concatenate([vg, w3[:, 2].reshape(d, nb, CONV_LANES)], axis=2).reshape(d, -1)
    out, *cast = pl.pallas_call(
        functools.partial(_mix_e_kernel, n_cast=len(casts), tm=tm),
        grid=(b, ns),
        in_specs=[
            pl.BlockSpec((None, tm, d), lambda i, j: (i, j, 0)),
            _resident((1, d)),
            _resident(w_in.shape),
            _resident(conv_w.shape),
            _resident((1, D_CONV)),
            _resident((1, D_CONV)),
            _resident((1, D_CONV)),
            _resident(w_pool.shape),
            _resident((1, D_POOL)),
            _resident(w_out.shape),
        ] + [c.in_spec for c in casts],
        out_specs=[pl.BlockSpec((None, tm, d), lambda i, j: (i, j, 0))]
        + [c.out_spec for c in casts],
        out_shape=[jax.ShapeDtypeStruct((b, s, d), _F32)] + [c.out_shape for c in casts],
        scratch_shapes=[
            pltpu.VMEM((CONV_HALO + tm, D_CONV), _F32),
            pltpu.VMEM((POOL_HALO + tm, D_POOL), _F32),
            pltpu.VMEM((tm, D_CONV), _F32),
            pltpu.VMEM((tm, D_CONV + D_POOL), _BF16),
        ],
        compiler_params=pltpu.CompilerParams(
            dimension_semantics=("arbitrary", "arbitrary"),
            vmem_limit_bytes=VMEM_LIMIT),
        name="mix_e",
    )(x, norm_g.reshape(1, d), w_in, conv_w, conv_b.reshape(1, -1), ln_g.reshape(1, -1),
      ln_b.reshape(1, -1), w_pool, pool_scale.reshape(1, -1), w_out,
      *[w for w, _ in cast_weights])
    return out, cast


def _mix_o_kernel(*refs, n_cast, tm):
    x_ref, g_ref, wb_ref, wc_ref, wv_ref, cw_ref, wout_ref = refs[:7]
    cast_in = refs[7:7 + n_cast]
    o_ref = refs[7 + n_cast]
    cast_out = refs[8 + n_cast:8 + 2 * n_cast]
    hn_ref, hist_ref, ext_ref = refs[8 + 2 * n_cast:]
    s = pl.program_id(1)
    j = pl.program_id(2)

    @pl.when(j == 0)
    def _():
        x = x_ref[...]
        o_ref[...] = x
        hn_ref[...] = _rms(x, g_ref[...]).astype(_BF16)

    @pl.when(s == 0)
    def _():
        hist_ref[j] = jnp.zeros(hist_ref.shape[1:], _F32)

    _cast_side_job(cast_in, cast_out)
    hn = hn_ref[...]
    gate_b = jnp.dot(hn, wb_ref[...], preferred_element_type=_F32)
    gate_c = jnp.dot(hn, wc_ref[...], preferred_element_type=_F32)
    v = jnp.dot(hn, wv_ref[...], preferred_element_type=_F32)
    cv = gate_c * v
    ext_ref[0:SHORT_HALO, :] = hist_ref[j]
    ext_ref[SHORT_HALO:SHORT_HALO + tm, :] = cv
    conv = cw_ref[SHORT_WIDTH - 1:SHORT_WIDTH, :] * cv
    for k in range(SHORT_WIDTH - 1):
        off = SHORT_HALO - (SHORT_WIDTH - 1) + k
        conv = conv + cw_ref[k:k + 1, :] * ext_ref[off:off + tm, :]
    hist_ref[j] = ext_ref[tm:tm + SHORT_HALO, :]
    y = (gate_b * conv).astype(_BF16)
    o_ref[...] += jnp.dot(y, wout_ref[...], preferred_element_type=_F32)


def _mix_o_call(x, norm_g, w_in, conv_w, w_out, cast_weights):
    b, s, d = x.shape
    tm, tc = TM_MIX_O, TC_MIX_O
    nc = D_SHORT // tc
    ns = s // tm
    casts = [_cast_blocks(w, layer, b * ns * nc, lambda i, j, c: (i * ns + j) * nc + c)
             for w, layer in cast_weights]
    out, *cast = pl.pallas_call(
        functools.partial(_mix_o_kernel, n_cast=len(casts), tm=tm),
        grid=(b, ns, nc),
        in_specs=[
            pl.BlockSpec((None, tm, d), lambda i, j, c: (i, j, 0)),
            _resident((1, d)),
            pl.BlockSpec((d, tc), lambda i, j, c: (0, c)),
            pl.BlockSpec((d, tc), lambda i, j, c: (0, nc + c)),
            pl.BlockSpec((d, tc), lambda i, j, c: (0, 2 * nc + c)),
            pl.BlockSpec((SHORT_WIDTH, tc), lambda i, j, c: (0, c)),
            pl.BlockSpec((tc, d), lambda i, j, c: (c, 0)),
        ] + [c.in_spec for c in casts],
        out_specs=[pl.BlockSpec((None, tm, d), lambda i, j, c: (i, j, 0))]
        + [c.out_spec for c in casts],
        out_shape=[jax.ShapeDtypeStruct((b, s, d), _F32)] + [c.out_shape for c in casts],
        scratch_shapes=[
            pltpu.VMEM((tm, d), _BF16),
            pltpu.VMEM((nc, SHORT_HALO, tc), _F32),
            pltpu.VMEM((SHORT_HALO + tm, tc), _F32),
        ],
        compiler_params=pltpu.CompilerParams(
            dimension_semantics=("arbitrary", "arbitrary", "arbitrary"),
            vmem_limit_bytes=VMEM_LIMIT),
        name="mix_o",
    )(x, norm_g.reshape(1, d), w_in, w_in, w_in, conv_w, w_out, *[w for w, _ in cast_weights])
    return out, cast


def _ffn_kernel(*refs, n_cast, final_norm):
    x_ref, g_ref, wg_ref, wu_ref, wd_ref, fg_ref = refs[:6]
    cast_in = refs[6:6 + n_cast]
    o_ref = refs[6 + n_cast]
    cast_out = refs[7 + n_cast:7 + 2 * n_cast]
    (hn_ref,) = refs[7 + 2 * n_cast:]
    f = pl.program_id(1)

    @pl.when(f == 0)
    def _():
        h = x_ref[...]
        o_ref[...] = h
        hn_ref[...] = _rms(h, g_ref[...]).astype(_BF16)

    _cast_side_job(cast_in, cast_out)
    hn = hn_ref[...]
    gate = jnp.dot(hn, wg_ref[...], preferred_element_type=_F32)
    up = jnp.dot(hn, wu_ref[...], preferred_element_type=_F32)
    z = (gate * _sigmoid(gate) * up).astype(_BF16)
    o_ref[...] += jnp.dot(z, wd_ref[...], preferred_element_type=_F32)

    if final_norm:
        @pl.when(f == pl.num_programs(1) - 1)
        def _():
            o_ref[...] = _rms(o_ref[...], fg_ref[...])


def _ffn_call(x, norm_g, w_gate, w_up, w_down, final_g, cast_weights, *, final_norm):
    b, s, d = x.shape
    t = b * s
    tm, tf = TM_FFN, TF_FFN
    nf = D_FF // tf
    casts = [_cast_blocks(w, layer, (t // tm) * nf, lambda i, f: i * nf + f)
             for w, layer in cast_weights]
    out, *cast = pl.pallas_call(
        functools.partial(_ffn_kernel, n_cast=len(casts), final_norm=final_norm),
        grid=(t // tm, nf),
        in_specs=[
            pl.BlockSpec((tm, d), lambda i, f: (i, 0)),
            _resident((1, d)),
            pl.BlockSpec((d, tf), lambda i, f: (0, f)),
            pl.BlockSpec((d, tf), lambda i, f: (0, f)),
            pl.BlockSpec((tf, d), lambda i, f: (f, 0)),
            _resident((1, d)),
        ] + [c.in_spec for c in casts],
        out_specs=[pl.BlockSpec((tm, d), lambda i, f: (i, 0))] + [c.out_spec for c in casts],
        out_shape=[jax.ShapeDtypeStruct((t, d), _F32)] + [c.out_shape for c in casts],
        scratch_shapes=[pltpu.VMEM((tm, d), _BF16)],
        compiler_params=pltpu.CompilerParams(
            dimension_semantics=("arbitrary", "arbitrary"),
            vmem_limit_bytes=VMEM_LIMIT),
        name="ffn_final" if final_norm else "ffn",
    )(x.reshape(t, d), norm_g.reshape(1, d), w_gate, w_up, w_down, final_g.reshape(1, d),
      *[w for w, _ in cast_weights])
    return out.reshape(b, s, d), cast


def kernel(x, mix_norm_e, w_in_e, conv_w_e, conv_b_e, ln_g_e, ln_b_e, w_pool_e, pool_scale_e,
           w_out_e, mix_norm_o, w_in_o, conv_w_o, w_out_o, ffn_norm, w_gate, w_up, w_down,
           final_norm):
    bf = lambda w: w.astype(_BF16)
    h, (wg0, wu0, wd0, wi1, wo1) = _mix_e_call(
        x, mix_norm_e[0], bf(w_in_e[0]), conv_w_e[0], conv_b_e[0], ln_g_e[0], ln_b_e[0],
        bf(w_pool_e[0]), pool_scale_e[0], bf(w_out_e[0]),
        cast_weights=[(w_gate, 0), (w_up, 0), (w_down, 0), (w_in_o, 0), (w_out_o, 0)])
    h, _ = _ffn_call(h, ffn_norm[0], wg0, wu0, wd0, final_norm, cast_weights=[],
                     final_norm=False)
    h, (wg1, wu1, wd1) = _mix_o_call(h, mix_norm_o[0], wi1, conv_w_o[0], wo1,
                                     cast_weights=[(w_gate, 1), (w_up, 1), (w_down, 1)])
    out, _ = _ffn_call(h, ffn_norm[1], wg1, wu1, wd1, final_norm, cast_weights=[],
                       final_norm=True)
    return out
```

```python
import functools
from typing import NamedTuple

import jax
import jax.numpy as jnp
from jax import lax
from jax.experimental import pallas as pl
from jax.experimental.pallas import tpu as pltpu

D_MODEL = 2048
D_CONV = D_MODEL // 2
CONV_WIDTH = 31
D_POOL = D_MODEL // 2
POOL_WINDOWS = (2, 4, 8, 16)
POOL_GROUP = D_POOL // len(POOL_WINDOWS)
D_SHORT = D_MODEL
SHORT_WIDTH = 3
D_FF = -(-8 * D_MODEL // (3 * 256)) * 256
EPS = 1e-6

SUBLANES = 8
LANES = 128
BF16_ROWS = 16
CONV_HALO = 32
POOL_HALO = 16
SHORT_HALO = 8

TM_MIX_E = 256
TM_MIX_O = 512
TC_MIX_O = 512
TM_FFN = 1024
TF_FFN = 512
CONV_ROWS = 64
CONV_LANES = POOL_GROUP
NORM_ROWS = 32
VMEM_LIMIT = 58 * 1024 * 1024

_F32 = jnp.float32
_BF16 = jnp.bfloat16


def _rms(xf, g):
    return xf * lax.rsqrt(jnp.mean(xf * xf, axis=-1, keepdims=True) + EPS) * g


def _sigmoid(x):
    return 1.0 / (1.0 + jnp.exp(-x))


def _resident(shape):
    nd = len(shape)
    return pl.BlockSpec(shape, lambda *_: (0,) * nd, pipeline_mode=pl.Buffered(1))


class _CastBlocks(NamedTuple):
    in_spec: pl.BlockSpec
    out_spec: pl.BlockSpec
    out_shape: jax.ShapeDtypeStruct


def _cast_blocks(w, layer, n_steps, step_of):
    _, rows, cols = w.shape
    blk = next(r for r in range(BF16_ROWS, rows + 1, BF16_ROWS)
               if rows % r == 0 and rows // r <= n_steps)
    last = rows // blk - 1
    return _CastBlocks(
        pl.BlockSpec((None, blk, cols), lambda *g: (layer, jnp.minimum(step_of(*g), last), 0)),
        pl.BlockSpec((blk, cols), lambda *g: (jnp.minimum(step_of(*g), last), 0)),
        jax.ShapeDtypeStruct((rows, cols), _BF16))


def _cast_side_job(cast_in, cast_out):
    for src, dst in zip(cast_in, cast_out):
        dst[...] = src[...].astype(_BF16)


def _mix_e_kernel(*refs, n_cast, tm):
    (x_ref, g_ref, win_ref, cw_ref, cb_ref, lng_ref, lnb_ref, wp_ref, ps_ref,
     wout_ref) = refs[:10]
    cast_in = refs[10:10 + n_cast]
    o_ref = refs[10 + n_cast]
    cast_out = refs[11 + n_cast:11 + 2 * n_cast]
    ext_a, ext_b, c_ref, y_ref = refs[11 + 2 * n_cast:]
    s = pl.program_id(1)

    @pl.when(s == 0)
    def _():
        ext_a[0:CONV_HALO, :] = jnp.zeros((CONV_HALO, D_CONV), _F32)
        ext_b[0:POOL_HALO, :] = jnp.zeros((POOL_HALO, D_POOL), _F32)

    _cast_side_job(cast_in, cast_out)
    hn = _rms(x_ref[...], g_ref[...]).astype(_BF16)
    pos = s * tm + lax.broadcasted_iota(jnp.int32, (tm, 1), 0) + 1

    slab_rows = CONV_ROWS + CONV_HALO
    first_off = CONV_HALO - (CONV_WIDTH - 1)

    for n in range(D_CONV // CONV_LANES):
        lanes = slice(n * CONV_LANES, (n + 1) * CONV_LANES)
        for h in range(CONV_LANES // LANES):
            c0 = n * CONV_LANES + h * LANES
            w_vg = jnp.concatenate([win_ref[:, c0:c0 + LANES],
                                    win_ref[:, D_CONV + c0:D_CONV + c0 + LANES]], axis=1)
            u = jnp.dot(hn, w_vg, preferred_element_type=_F32)
            ext_a[CONV_HALO:CONV_HALO + tm, c0:c0 + LANES] = u[:, :LANES] * _sigmoid(u[:, LANES:])
        ext_b[POOL_HALO:POOL_HALO + tm, lanes] = jnp.dot(
            hn, win_ref[:, 2 * D_CONV + n * CONV_LANES:2 * D_CONV + (n + 1) * CONV_LANES],
            preferred_element_type=_F32)

        for h in range(CONV_LANES // LANES):
            cl = slice(n * CONV_LANES + h * LANES, n * CONV_LANES + (h + 1) * LANES)
            for r0 in range(0, tm, CONV_ROWS):
                slab = ext_a[r0:r0 + slab_rows, cl]
                acc = jnp.zeros((CONV_ROWS, LANES), _F32)
                for r in range(SUBLANES):
                    rot = slab if r == 0 else pltpu.roll(slab, slab_rows - r, axis=0)
                    for q in range(CONV_HALO // SUBLANES + 1):
                        j = SUBLANES * q + r - first_off
                        if 0 <= j < CONV_WIDTH:
                            tap = rot[SUBLANES * q:SUBLANES * q + CONV_ROWS, :]
                            acc = acc + cw_ref[j:j + 1, cl] * tap
                c_ref[r0:r0 + CONV_ROWS, cl] = acc

        w = POOL_WINDOWS[n]
        xe = ext_b[:, lanes]
        ssum = xe
        span = 1
        while span < w:
            ssum = ssum + pltpu.roll(ssum, span, axis=0)
            span *= 2
        xg = xe[POOL_HALO:, :]
        cnt = jnp.minimum(pos, w).astype(_F32)
        p = (ssum[POOL_HALO:, :] / cnt - xg).astype(_BF16)
        q = jnp.dot(p, wp_ref[n], preferred_element_type=_F32) * ps_ref[:, lanes]
        y_ref[:, D_CONV + n * POOL_GROUP:D_CONV + (n + 1) * POOL_GROUP] = q.astype(_BF16)

    o_ref[...] = x_ref[...] + jnp.dot(y_ref[:, D_CONV:], wout_ref[D_CONV:, :],
                                      preferred_element_type=_F32)

    for r0 in range(0, tm, NORM_ROWS):
        c = c_ref[r0:r0 + NORM_ROWS, :] + cb_ref[...]
        mu = jnp.mean(c, axis=-1, keepdims=True)
        xc = c - mu
        var = jnp.mean(xc * xc, axis=-1, keepdims=True)
        ln = xc * lax.rsqrt(var + EPS) * lng_ref[...] + lnb_ref[...]
        y_ref[r0:r0 + NORM_ROWS, 0:D_CONV] = (ln * _sigmoid(ln)).astype(_BF16)

    o_ref[...] += jnp.dot(y_ref[:, 0:D_CONV], wout_ref[0:D_CONV, :],
                          preferred_element_type=_F32)

    ext_a[0:CONV_HALO, :] = ext_a[tm:tm + CONV_HALO, :]
    ext_b[0:POOL_HALO, :] = ext_b[tm:tm + POOL_HALO, :]


def _mix_e_call(x, norm_g, w_in, conv_w, conv_b, ln_g, ln_b, w_pool, pool_scale, w_out,
                cast_weights):
    b, s, d = x.shape
    tm = TM_MIX_E
    ns = s // tm
    casts = [_cast_blocks(w, layer, b * ns, lambda i, j: i * ns + j)
             for w, layer in cast_weights]
    out, *cast = pl.pallas_call(
        functools.partial(_mix_e_kernel, n_cast=len(casts), tm=tm),
        grid=(b, ns),
        in_specs=[
            pl.BlockSpec((None, tm, d), lambda i, j: (i, j, 0)),
            _resident((1, d)),
            _resident(w_in.shape),
            _resident(conv_w.shape),
            _resident((1, D_CONV)),
            _resident((1, D_CONV)),
            _resident((1, D_CONV)),
            _resident(w_pool.shape),
            _resident((1, D_POOL)),
            _resident(w_out.shape),
        ] + [c.in_spec for c in casts],
        out_specs=[pl.BlockSpec((None, tm, d), lambda i, j: (i, j, 0))]
        + [c.out_spec for c in casts],
        out_shape=[jax.ShapeDtypeStruct((b, s, d), _F32)] + [c.out_shape for c in casts],
        scratch_shapes=[
            pltpu.VMEM((CONV_HALO + tm, D_CONV), _F32),
            pltpu.VMEM((POOL_HALO + tm, D_POOL), _F32),
            pltpu.VMEM((tm, D_CONV), _F32),
            pltpu.VMEM((tm, D_CONV + D_POOL), _BF16),
        ],
        compiler_params=pltpu.CompilerParams(
            dimension_semantics=("arbitrary", "arbitrary"),
            vmem_limit_bytes=VMEM_LIMIT),
        name="mix_e",
    )(x, norm_g.reshape(1, d), w_in, conv_w, conv_b.reshape(1, -1), ln_g.reshape(1, -1),
      ln_b.reshape(1, -1), w_pool, pool_scale.reshape(1, -1), w_out,
      *[w for w, _ in cast_weights])
    return out, cast


def _mix_o_kernel(*refs, n_cast, tm):
    x_ref, g_ref, wb_ref, wc_ref, wv_ref, cw_ref, wout_ref = refs[:7]
    cast_in = refs[7:7 + n_cast]
    o_ref = refs[7 + n_cast]
    cast_out = refs[8 + n_cast:8 + 2 * n_cast]
    hn_ref, hist_ref, ext_ref = refs[8 + 2 * n_cast:]
    s = pl.program_id(1)
    j = pl.program_id(2)

    @pl.when(j == 0)
    def _():
        x = x_ref[...]
        o_ref[...] = x
        hn_ref[...] = _rms(x, g_ref[...]).astype(_BF16)

    @pl.when(s == 0)
    def _():
        hist_ref[j] = jnp.zeros(hist_ref.shape[1:], _F32)

    _cast_side_job(cast_in, cast_out)
    hn = hn_ref[...]
    gate_c = jnp.dot(hn, wc_ref[...], preferred_element_type=_F32)
    v = jnp.dot(hn, wv_ref[...], preferred_element_type=_F32)
    gate_b = jnp.dot(hn, wb_ref[...], preferred_element_type=_F32)
    cv = gate_c * v
    ext_ref[0:SHORT_HALO, :] = hist_ref[j]
    ext_ref[SHORT_HALO:SHORT_HALO + tm, :] = cv
    conv = cw_ref[SHORT_WIDTH - 1:SHORT_WIDTH, :] * cv
    for k in range(SHORT_WIDTH - 1):
        off = SHORT_HALO - (SHORT_WIDTH - 1) + k
        conv = conv + cw_ref[k:k + 1, :] * ext_ref[off:off + tm, :]
    hist_ref[j] = ext_ref[tm:tm + SHORT_HALO, :]
    y = (gate_b * conv).astype(_BF16)
    o_ref[...] += jnp.dot(y, wout_ref[...], preferred_element_type=_F32)


def _mix_o_call(x, norm_g, w_in, conv_w, w_out, cast_weights):
    b, s, d = x.shape
    tm, tc = TM_MIX_O, TC_MIX_O
    nc = D_SHORT // tc
    ns = s // tm
    casts = [_cast_blocks(w, layer, b * ns * nc, lambda i, j, c: (i * ns + j) * nc + c)
             for w, layer in cast_weights]
    out, *cast = pl.pallas_call(
        functools.partial(_mix_o_kernel, n_cast=len(casts), tm=tm),
        grid=(b, ns, nc),
        in_specs=[
            pl.BlockSpec((None, tm, d), lambda i, j, c: (i, j, 0)),
            _resident((1, d)),
            pl.BlockSpec((d, tc), lambda i, j, c: (0, c)),
            pl.BlockSpec((d, tc), lambda i, j, c: (0, nc + c)),
            pl.BlockSpec((d, tc), lambda i, j, c: (0, 2 * nc + c)),
            pl.BlockSpec((SHORT_WIDTH, tc), lambda i, j, c: (0, c)),
            pl.BlockSpec((tc, d), lambda i, j, c: (c, 0)),
        ] + [c.in_spec for c in casts],
        out_specs=[pl.BlockSpec((None, tm, d), lambda i, j, c: (i, j, 0))]
        + [c.out_spec for c in casts],
        out_shape=[jax.ShapeDtypeStruct((b, s, d), _F32)] + [c.out_shape for c in casts],
        scratch_shapes=[
            pltpu.VMEM((tm, d), _BF16),
            pltpu.VMEM((nc, SHORT_HALO, tc), _F32),
            pltpu.VMEM((SHORT_HALO + tm, tc), _F32),
        ],
        compiler_params=pltpu.CompilerParams(
            dimension_semantics=("arbitrary", "arbitrary", "arbitrary"),
            vmem_limit_bytes=VMEM_LIMIT),
        name="mix_o",
    )(x, norm_g.reshape(1, d), w_in, w_in, w_in, conv_w, w_out, *[w for w, _ in cast_weights])
    return out, cast


def _ffn_kernel(*refs, n_cast, final_norm):
    x_ref, g_ref, wg_ref, wu_ref, wd_ref, fg_ref = refs[:6]
    cast_in = refs[6:6 + n_cast]
    o_ref = refs[6 + n_cast]
    cast_out = refs[7 + n_cast:7 + 2 * n_cast]
    (hn_ref,) = refs[7 + 2 * n_cast:]
    f = pl.program_id(1)

    @pl.when(f == 0)
    def _():
        h = x_ref[...]
        o_ref[...] = h
        hn_ref[...] = _rms(h, g_ref[...]).astype(_BF16)

    _cast_side_job(cast_in, cast_out)
    hn = hn_ref[...]
    gate = jnp.dot(hn, wg_ref[...], preferred_element_type=_F32)
    up = jnp.dot(hn, wu_ref[...], preferred_element_type=_F32)
    z = (gate * _sigmoid(gate) * up).astype(_BF16)
    o_ref[...] += jnp.dot(z, wd_ref[...], preferred_element_type=_F32)

    if final_norm:
        @pl.when(f == pl.num_programs(1) - 1)
        def _():
            o_ref[...] = _rms(o_ref[...], fg_ref[...])


def _ffn_call(x, norm_g, w_gate, w_up, w_down, final_g, cast_weights, *, final_norm):
    b, s, d = x.shape
    t = b * s
    tm, tf = TM_FFN, TF_FFN
    nf = D_FF // tf
    casts = [_cast_blocks(w, layer, (t // tm) * nf, lambda i, f: i * nf + f)
             for w, layer in cast_weights]
    out, *cast = pl.pallas_call(
        functools.partial(_ffn_kernel, n_cast=len(casts), final_norm=final_norm),
        grid=(t // tm, nf),
        in_specs=[
            pl.BlockSpec((tm, d), lambda i, f: (i, 0)),
            _resident((1, d)),
            pl.BlockSpec((d, tf), lambda i, f: (0, f)),
            pl.BlockSpec((d, tf), lambda i, f: (0, f)),
            pl.BlockSpec((tf, d), lambda i, f: (f, 0)),
            _resident((1, d)),
        ] + [c.in_spec for c in casts],
        out_specs=[pl.BlockSpec((tm, d), lambda i, f: (i, 0))] + [c.out_spec for c in casts],
        out_shape=[jax.ShapeDtypeStruct((t, d), _F32)] + [c.out_shape for c in casts],
        scratch_shapes=[pltpu.VMEM((tm, d), _BF16)],
        compiler_params=pltpu.CompilerParams(
            dimension_semantics=("arbitrary", "arbitrary"),
            vmem_limit_bytes=VMEM_LIMIT),
        name="ffn_final" if final_norm else "ffn",
    )(x.reshape(t, d), norm_g.reshape(1, d), w_gate, w_up, w_down, final_g.reshape(1, d),
      *[w for w, _ in cast_weights])
    return out.reshape(b, s, d), cast


def kernel(x, mix_norm_e, w_in_e, conv_w_e, conv_b_e, ln_g_e, ln_b_e, w_pool_e, pool_scale_e,
           w_out_e, mix_norm_o, w_in_o, conv_w_o, w_out_o, ffn_norm, w_gate, w_up, w_down,
           final_norm):
    bf = lambda w: w.astype(_BF16)
    h, (wg0, wu0, wd0, wi1, wo1) = _mix_e_call(
        x, mix_norm_e[0], bf(w_in_e[0]), conv_w_e[0], conv_b_e[0], ln_g_e[0], ln_b_e[0],
        bf(w_pool_e[0]), pool_scale_e[0], bf(w_out_e[0]),
        cast_weights=[(w_gate, 0), (w_up, 0), (w_down, 0), (w_in_o, 0), (w_out_o, 0)])
    h, _ = _ffn_call(h, ffn_norm[0], wg0, wu0, wd0, final_norm, cast_weights=[],
                     final_norm=False)
    h, (wg1, wu1, wd1) = _mix_o_call(h, mix_norm_o[0], wi1, conv_w_o[0], wo1,
                                     cast_weights=[(w_gate, 1), (w_up, 1), (w_down, 1)])
    out, _ = _ffn_call(h, ffn_norm[1], wg1, wu1, wd1, final_norm, cast_weights=[],
                       final_norm=True)
    return out
```

```python
import functools
from typing import NamedTuple

import jax
import jax.numpy as jnp
from jax import lax
from jax.experimental import pallas as pl
from jax.experimental.pallas import tpu as pltpu

D_MODEL = 2048
D_CONV = D_MODEL // 2
CONV_WIDTH = 31
D_POOL = D_MODEL // 2
POOL_WINDOWS = (2, 4, 8, 16)
POOL_GROUP = D_POOL // len(POOL_WINDOWS)
D_SHORT = D_MODEL
SHORT_WIDTH = 3
D_FF = -(-8 * D_MODEL // (3 * 256)) * 256
EPS = 1e-6

SUBLANES = 8
LANES = 128
BF16_ROWS = 16
CONV_HALO = 32
POOL_HALO = 16
SHORT_HALO = 8

TM_MIX_E = 256
TM_MIX_O = 512
TC_MIX_O = 512
TM_FFN = 1024
TF_FFN = 512
CONV_ROWS = 64
CONV_LANES = POOL_GROUP
NORM_ROWS = 32
VMEM_LIMIT = 58 * 1024 * 1024

_F32 = jnp.float32
_BF16 = jnp.bfloat16


def _rms(xf, g):
    return xf * lax.rsqrt(jnp.mean(xf * xf, axis=-1, keepdims=True) + EPS) * g


def _sigmoid(x):
    return 1.0 / (1.0 + jnp.exp(-x))


def _resident(shape):
    nd = len(shape)
    return pl.BlockSpec(shape, lambda *_: (0,) * nd, pipeline_mode=pl.Buffered(1))


class _CastBlocks(NamedTuple):
    in_spec: pl.BlockSpec
    out_spec: pl.BlockSpec
    out_shape: jax.ShapeDtypeStruct


def _cast_blocks(w, layer, n_steps, step_of):
    _, rows, cols = w.shape
    blk = next(r for r in range(BF16_ROWS, rows + 1, BF16_ROWS)
               if rows % r == 0 and rows // r <= n_steps)
    last = rows // blk - 1
    return _CastBlocks(
        pl.BlockSpec((None, blk, cols), lambda *g: (layer, jnp.minimum(step_of(*g), last), 0)),
        pl.BlockSpec((blk, cols), lambda *g: (jnp.minimum(step_of(*g), last), 0)),
        jax.ShapeDtypeStruct((rows, cols), _BF16))


def _cast_side_job(cast_in, cast_out):
    for src, dst in zip(cast_in, cast_out):
        dst[...] = src[...].astype(_BF16)


def _mix_e_kernel(*refs, n_cast, tm):
    (x_ref, g_ref, win_ref, cw_ref, cb_ref, lng_ref, lnb_ref, wp_ref, ps_ref,
     wout_ref) = refs[:10]
    cast_in = refs[10:10 + n_cast]
    o_ref = refs[10 + n_cast]
    cast_out = refs[11 + n_cast:11 + 2 * n_cast]
    ext_a, ext_b, c_ref, y_ref = refs[11 + 2 * n_cast:]
    s = pl.program_id(1)

    @pl.when(s == 0)
    def _():
        ext_a[0:CONV_HALO, :] = jnp.zeros((CONV_HALO, D_CONV), _F32)
        ext_b[0:POOL_HALO, :] = jnp.zeros((POOL_HALO, D_POOL), _F32)

    _cast_side_job(cast_in, cast_out)
    hn = _rms(x_ref[...], g_ref[...]).astype(_BF16)
    pos = s * tm + lax.broadcasted_iota(jnp.int32, (tm, 1), 0) + 1

    slab_rows = CONV_ROWS + CONV_HALO
    first_off = CONV_HALO - (CONV_WIDTH - 1)

    def glu_block(n):
        for h in range(CONV_LANES // LANES):
            c0 = n * CONV_LANES + h * LANES
            w_vg = jnp.concatenate([win_ref[:, c0:c0 + LANES],
                                    win_ref[:, D_CONV + c0:D_CONV + c0 + LANES]], axis=1)
            u = jnp.dot(hn, w_vg, preferred_element_type=_F32)
            ext_a[CONV_HALO:CONV_HALO + tm, c0:c0 + LANES] = u[:, :LANES] * _sigmoid(u[:, LANES:])

    def conv_block(n):
        for h in range(CONV_LANES // LANES):
            cl = slice(n * CONV_LANES + h * LANES, n * CONV_LANES + (h + 1) * LANES)
            for r0 in range(0, tm, CONV_ROWS):
                slab = ext_a[r0:r0 + slab_rows, cl]
                acc = jnp.zeros((CONV_ROWS, LANES), _F32)
                for r in range(SUBLANES):
                    rot = slab if r == 0 else pltpu.roll(slab, slab_rows - r, axis=0)
                    for q in range(CONV_HALO // SUBLANES + 1):
                        j = SUBLANES * q + r - first_off
                        if 0 <= j < CONV_WIDTH:
                            tap = rot[SUBLANES * q:SUBLANES * q + CONV_ROWS, :]
                            acc = acc + cw_ref[j:j + 1, cl] * tap
                c_ref[r0:r0 + CONV_ROWS, cl] = acc

    def pool_branch():
        ext_b[POOL_HALO:POOL_HALO + tm, :] = jnp.dot(hn, win_ref[:, 2 * D_CONV:],
                                                     preferred_element_type=_F32)
        for n, w in enumerate(POOL_WINDOWS):
            lanes = slice(n * POOL_GROUP, (n + 1) * POOL_GROUP)
            xe = ext_b[:, lanes]
            ssum = xe
            span = 1
            while span < w:
                ssum = ssum + pltpu.roll(ssum, span, axis=0)
                span *= 2
            xg = xe[POOL_HALO:, :]
            cnt = jnp.minimum(pos, w).astype(_F32)
            p = (ssum[POOL_HALO:, :] / cnt - xg).astype(_BF16)
            q = jnp.dot(p, wp_ref[n], preferred_element_type=_F32) * ps_ref[:, lanes]
            y_ref[:, D_CONV + n * POOL_GROUP:D_CONV + (n + 1) * POOL_GROUP] = q.astype(_BF16)

    glu_block(0)
    pool_branch()
    conv_block(0)
    o_ref[...] = x_ref[...] + jnp.dot(y_ref[:, D_CONV:], wout_ref[D_CONV:, :],
                                      preferred_element_type=_F32)
    for n in range(1, D_CONV // CONV_LANES):
        glu_block(n)
        conv_block(n)

    for r0 in range(0, tm, NORM_ROWS):
        c = c_ref[r0:r0 + NORM_ROWS, :] + cb_ref[...]
        mu = jnp.mean(c, axis=-1, keepdims=True)
        xc = c - mu
        var = jnp.mean(xc * xc, axis=-1, keepdims=True)
        ln = xc * lax.rsqrt(var + EPS) * lng_ref[...] + lnb_ref[...]
        y_ref[r0:r0 + NORM_ROWS, 0:D_CONV] = (ln * _sigmoid(ln)).astype(_BF16)

    o_ref[...] += jnp.dot(y_ref[:, 0:D_CONV], wout_ref[0:D_CONV, :],
                          preferred_element_type=_F32)

    ext_a[0:CONV_HALO, :] = ext_a[tm:tm + CONV_HALO, :]
    ext_b[0:POOL_HALO, :] = ext_b[tm:tm + POOL_HALO, :]


def _mix_e_call(x, norm_g, w_in, conv_w, conv_b, ln_g, ln_b, w_pool, pool_scale, w_out,
                cast_weights):
    b, s, d = x.shape
    tm = TM_MIX_E
    ns = s // tm
    casts = [_cast_blocks(w, layer, b * ns, lambda i, j: i * ns + j)
             for w, layer in cast_weights]
    out, *cast = pl.pallas_call(
        functools.partial(_mix_e_kernel, n_cast=len(casts), tm=tm),
        grid=(b, ns),
        in_specs=[
            pl.BlockSpec((None, tm, d), lambda i, j: (i, j, 0)),
            _resident((1, d)),
            _resident(w_in.shape),
            _resident(conv_w.shape),
            _resident((1, D_CONV)),
            _resident((1, D_CONV)),
            _resident((1, D_CONV)),
            _resident(w_pool.shape),
            _resident((1, D_POOL)),
            _resident(w_out.shape),
        ] + [c.in_spec for c in casts],
        out_specs=[pl.BlockSpec((None, tm, d), lambda i, j: (i, j, 0))]
        + [c.out_spec for c in casts],
        out_shape=[jax.ShapeDtypeStruct((b, s, d), _F32)] + [c.out_shape for c in casts],
        scratch_shapes=[
            pltpu.VMEM((CONV_HALO + tm, D_CONV), _F32),
            pltpu.VMEM((POOL_HALO + tm, D_POOL), _F32),
            pltpu.VMEM((tm, D_CONV), _F32),
            pltpu.VMEM((tm, D_CONV + D_POOL), _BF16),
        ],
        compiler_params=pltpu.CompilerParams(
            dimension_semantics=("arbitrary", "arbitrary"),
            vmem_limit_bytes=VMEM_LIMIT),
        name="mix_e",
    )(x, norm_g.reshape(1, d), w_in, conv_w, conv_b.reshape(1, -1), ln_g.reshape(1, -1),
      ln_b.reshape(1, -1), w_pool, pool_scale.reshape(1, -1), w_out,
      *[w for w, _ in cast_weights])
    return out, cast


def _mix_o_kernel(*refs, n_cast, tm):
    x_ref, g_ref, wb_ref, wc_ref, wv_ref, cw_ref, wout_ref = refs[:7]
    cast_in = refs[7:7 + n_cast]
    o_ref = refs[7 + n_cast]
    cast_out = refs[8 + n_cast:8 + 2 * n_cast]
    hn_ref, hist_ref, ext_ref = refs[8 + 2 * n_cast:]
    s = pl.program_id(1)
    j = pl.program_id(2)

    @pl.when(j == 0)
    def _():
        x = x_ref[...]
        o_ref[...] = x
        hn_ref[...] = _rms(x, g_ref[...]).astype(_BF16)

    @pl.when(s == 0)
    def _():
        hist_ref[j] = jnp.zeros(hist_ref.shape[1:], _F32)

    _cast_side_job(cast_in, cast_out)
    hn = hn_ref[...]
    gate_c = jnp.dot(hn, wc_ref[...], preferred_element_type=_F32)
    v = jnp.dot(hn, wv_ref[...], preferred_element_type=_F32)
    gate_b = jnp.dot(hn, wb_ref[...], preferred_element_type=_F32)
    cv = gate_c * v
    ext_ref[0:SHORT_HALO, :] = hist_ref[j]
    ext_ref[SHORT_HALO:SHORT_HALO + tm, :] = cv
    conv = cw_ref[SHORT_WIDTH - 1:SHORT_WIDTH, :] * cv
    for k in range(SHORT_WIDTH - 1):
        off = SHORT_HALO - (SHORT_WIDTH - 1) + k
        conv = conv + cw_ref[k:k + 1, :] * ext_ref[off:off + tm, :]
    hist_ref[j] = ext_ref[tm:tm + SHORT_HALO, :]
    y = (gate_b * conv).astype(_BF16)
    o_ref[...] += jnp.dot(y, wout_ref[...], preferred_element_type=_F32)


def _mix_o_call(x, norm_g, w_in, conv_w, w_out, cast_weights):
    b, s, d = x.shape
    tm, tc = TM_MIX_O, TC_MIX_O
    nc = D_SHORT // tc
    ns = s // tm
    casts = [_cast_blocks(w, layer, b * ns * nc, lambda i, j, c: (i * ns + j) * nc + c)
             for w, layer in cast_weights]
    out, *cast = pl.pallas_call(
        functools.partial(_mix_o_kernel, n_cast=len(casts), tm=tm),
        grid=(b, ns, nc),
        in_specs=[
            pl.BlockSpec((None, tm, d), lambda i, j, c: (i, j, 0)),
            _resident((1, d)),
            pl.BlockSpec((d, tc), lambda i, j, c: (0, c)),
            pl.BlockSpec((d, tc), lambda i, j, c: (0, nc + c)),
            pl.BlockSpec((d, tc), lambda i, j, c: (0, 2 * nc + c)),
            pl.BlockSpec((SHORT_WIDTH, tc), lambda i, j, c: (0, c)),
            pl.BlockSpec((tc, d), lambda i, j, c: (c, 0)),
        ] + [c.in_spec for c in casts],
        out_specs=[pl.BlockSpec((None, tm, d), lambda i, j, c: (i, j, 0))]
        + [c.out_spec for c in casts],
        out_shape=[jax.ShapeDtypeStruct((b, s, d), _F32)] + [c.out_shape for c in casts],
        scratch_shapes=[
            pltpu.VMEM((tm, d), _BF16),
            pltpu.VMEM((nc, SHORT_HALO, tc), _F32),
            pltpu.VMEM((SHORT_HALO + tm, tc), _F32),
        ],
        compiler_params=pltpu.CompilerParams(
            dimension_semantics=("arbitrary", "arbitrary", "arbitrary"),
            vmem_limit_bytes=VMEM_LIMIT),
        name="mix_o",
    )(x, norm_g.reshape(1, d), w_in, w_in, w_in, conv_w, w_out, *[w for w, _ in cast_weights])
    return out, cast


def _ffn_kernel(*refs, n_cast, final_norm):
    x_ref, g_ref, wg_ref, wu_ref, wd_ref, fg_ref = refs[:6]
    cast_in = refs[6:6 + n_cast]
    o_ref = refs[6 + n_cast]
    cast_out = refs[7 + n_cast:7 + 2 * n_cast]
    (hn_ref,) = refs[7 + 2 * n_cast:]
    f = pl.program_id(1)

    @pl.when(f == 0)
    def _():
        h = x_ref[...]
        o_ref[...] = h
        hn_ref[...] = _rms(h, g_ref[...]).astype(_BF16)

    _cast_side_job(cast_in, cast_out)
    hn = hn_ref[...]
    half = wg_ref.shape[1] // 2
    zs = []
    for h in range(2):
        cols = slice(h * half, (h + 1) * half)
        gate = jnp.dot(hn, wg_ref[:, cols], preferred_element_type=_F32)
        up = jnp.dot(hn, wu_ref[:, cols], preferred_element_type=_F32)
        zs.append((gate * _sigmoid(gate) * up).astype(_BF16))
    for h in range(2):
        o_ref[...] += jnp.dot(zs[h], wd_ref[h * half:(h + 1) * half, :],
                              preferred_element_type=_F32)

    if final_norm:
        @pl.when(f == pl.num_programs(1) - 1)
        def _():
            o_ref[...] = _rms(o_ref[...], fg_ref[...])


def _ffn_call(x, norm_g, w_gate, w_up, w_down, final_g, cast_weights, *, final_norm):
    b, s, d = x.shape
    t = b * s
    tm, tf = TM_FFN, TF_FFN
    nf = D_FF // tf
    casts = [_cast_blocks(w, layer, (t // tm) * nf, lambda i, f: i * nf + f)
             for w, layer in cast_weights]
    out, *cast = pl.pallas_call(
        functools.partial(_ffn_kernel, n_cast=len(casts), final_norm=final_norm),
        grid=(t // tm, nf),
        in_specs=[
            pl.BlockSpec((tm, d), lambda i, f: (i, 0)),
            _resident((1, d)),
            pl.BlockSpec((d, tf), lambda i, f: (0, f)),
            pl.BlockSpec((d, tf), lambda i, f: (0, f)),
            pl.BlockSpec((tf, d), lambda i, f: (f, 0)),
            _resident((1, d)),
        ] + [c.in_spec for c in casts],
        out_specs=[pl.BlockSpec((tm, d), lambda i, f: (i, 0))] + [c.out_spec for c in casts],
        out_shape=[jax.ShapeDtypeStruct((t, d), _F32)] + [c.out_shape for c in casts],
        scratch_shapes=[pltpu.VMEM((tm, d), _BF16)],
        compiler_params=pltpu.CompilerParams(
            dimension_semantics=("arbitrary", "arbitrary"),
            vmem_limit_bytes=VMEM_LIMIT),
        name="ffn_final" if final_norm else "ffn",
    )(x.reshape(t, d), norm_g.reshape(1, d), w_gate, w_up, w_down, final_g.reshape(1, d),
      *[w for w, _ in cast_weights])
    return out.reshape(b, s, d), cast


def kernel(x, mix_norm_e, w_in_e, conv_w_e, conv_b_e, ln_g_e, ln_b_e, w_pool_e, pool_scale_e,
           w_out_e, mix_norm_o, w_in_o, conv_w_o, w_out_o, ffn_norm, w_gate, w_up, w_down,
           final_norm):
    bf = lambda w: w.astype(_BF16)
    h, (wg0, wu0, wd0, wi1, wo1) = _mix_e_call(
        x, mix_norm_e[0], bf(w_in_e[0]), conv_w_e[0], conv_b_e[0], ln_g_e[0], ln_b_e[0],
        bf(w_pool_e[0]), pool_scale_e[0], bf(w_out_e[0]),
        cast_weights=[(w_gate, 0), (w_up, 0), (w_down, 0), (w_in_o, 0), (w_out_o, 0)])
    h, _ = _ffn_call(h, ffn_norm[0], wg0, wu0, wd0, final_norm, cast_weights=[],
                     final_norm=False)
    h, (wg1, wu1, wd1) = _mix_o_call(h, mix_norm_o[0], wi1, conv_w_o[0], wo1,
                                     cast_weights=[(w_gate, 1), (w_up, 1), (w_down, 1)])
    out, _ = _ffn_call(h, ffn_norm[1], wg1, wu1, wd1, final_norm, cast_weights=[],
                       final_norm=True)
    return out
```

```python
import functools
from typing import NamedTuple

import jax
import jax.numpy as jnp
from jax import lax
from jax.experimental import pallas as pl
from jax.experimental.pallas import tpu as pltpu

D_MODEL = 2048
D_CONV = D_MODEL // 2
CONV_WIDTH = 31
D_POOL = D_MODEL // 2
POOL_WINDOWS = (2, 4, 8, 16)
POOL_GROUP = D_POOL // len(POOL_WINDOWS)
D_SHORT = D_MODEL
SHORT_WIDTH = 3
D_FF = -(-8 * D_MODEL // (3 * 256)) * 256
EPS = 1e-6

SUBLANES = 8
LANES = 128
BF16_ROWS = 16
CONV_HALO = 32
POOL_HALO = 16
SHORT_HALO = 8

TM_MIX_E = 256
TM_MIX_O = 512
TC_MIX_O = 512
TM_FFN = 1024
TF_FFN = 512
CONV_ROWS = 64
CONV_LANES = POOL_GROUP
NORM_ROWS = 32
VMEM_LIMIT = 58 * 1024 * 1024

_F32 = jnp.float32
_BF16 = jnp.bfloat16


def _rms(xf, g):
    return xf * lax.rsqrt(jnp.mean(xf * xf, axis=-1, keepdims=True) + EPS) * g


def _sigmoid(x):
    return 1.0 / (1.0 + jnp.exp(-x))


def _resident(shape):
    nd = len(shape)
    return pl.BlockSpec(shape, lambda *_: (0,) * nd, pipeline_mode=pl.Buffered(1))


class _CastBlocks(NamedTuple):
    in_spec: pl.BlockSpec
    out_spec: pl.BlockSpec
    out_shape: jax.ShapeDtypeStruct


def _cast_blocks(w, layer, n_steps, step_of):
    _, rows, cols = w.shape
    blk = next(r for r in range(BF16_ROWS, rows + 1, BF16_ROWS)
               if rows % r == 0 and rows // r <= n_steps)
    last = rows // blk - 1
    return _CastBlocks(
        pl.BlockSpec((None, blk, cols), lambda *g: (layer, jnp.minimum(step_of(*g), last), 0)),
        pl.BlockSpec((blk, cols), lambda *g: (jnp.minimum(step_of(*g), last), 0)),
        jax.ShapeDtypeStruct((rows, cols), _BF16))


def _cast_side_job(cast_in, cast_out):
    for src, dst in zip(cast_in, cast_out):
        dst[...] = src[...].astype(_BF16)


def _mix_e_kernel(*refs, n_cast, tm):
    (x_ref, g_ref, win_ref, cw_ref, cb_ref, lng_ref, lnb_ref, wp_ref, ps_ref,
     wout_ref) = refs[:10]
    cast_in = refs[10:10 + n_cast]
    o_ref = refs[10 + n_cast]
    cast_out = refs[11 + n_cast:11 + 2 * n_cast]
    ext_a, ext_b, c_ref, y_ref = refs[11 + 2 * n_cast:]
    s = pl.program_id(1)

    @pl.when(s == 0)
    def _():
        ext_a[0:CONV_HALO, :] = jnp.zeros((CONV_HALO, D_CONV), _F32)
        ext_b[0:POOL_HALO, :] = jnp.zeros((POOL_HALO, D_POOL), _F32)

    _cast_side_job(cast_in, cast_out)
    hn = _rms(x_ref[...], g_ref[...]).astype(_BF16)
    pos = s * tm + lax.broadcasted_iota(jnp.int32, (tm, 1), 0) + 1

    slab_rows = CONV_ROWS + CONV_HALO
    first_off = CONV_HALO - (CONV_WIDTH - 1)

    def glu_block(n):
        for h in range(CONV_LANES // LANES):
            c0 = n * CONV_LANES + h * LANES
            w_vg = jnp.concatenate([win_ref[:, c0:c0 + LANES],
                                    win_ref[:, D_CONV + c0:D_CONV + c0 + LANES]], axis=1)
            u = jnp.dot(hn, w_vg, preferred_element_type=_F32)
            ext_a[CONV_HALO:CONV_HALO + tm, c0:c0 + LANES] = u[:, :LANES] * _sigmoid(u[:, LANES:])

    def conv_block(n):
        for h in range(CONV_LANES // LANES):
            cl = slice(n * CONV_LANES + h * LANES, n * CONV_LANES + (h + 1) * LANES)
            for r0 in range(0, tm, CONV_ROWS):
                slab = ext_a[r0:r0 + slab_rows, cl]
                acc = jnp.zeros((CONV_ROWS, LANES), _F32)
                for r in range(SUBLANES):
                    rot = slab if r == 0 else pltpu.roll(slab, slab_rows - r, axis=0)
                    for q in range(CONV_HALO // SUBLANES + 1):
                        j = SUBLANES * q + r - first_off
                        if 0 <= j < CONV_WIDTH:
                            tap = rot[SUBLANES * q:SUBLANES * q + CONV_ROWS, :]
                            acc = acc + cw_ref[j:j + 1, cl] * tap
                c_ref[r0:r0 + CONV_ROWS, cl] = acc

    def pool_branch():
        ext_b[POOL_HALO:POOL_HALO + tm, :] = jnp.dot(hn, win_ref[:, 2 * D_CONV:],
                                                     preferred_element_type=_F32)
        for n, w in enumerate(POOL_WINDOWS):
            lanes = slice(n * POOL_GROUP, (n + 1) * POOL_GROUP)
            xe = ext_b[:, lanes]
            ssum = xe
            span = 1
            while span < w:
                ssum = ssum + pltpu.roll(ssum, span, axis=0)
                span *= 2
            xg = xe[POOL_HALO:, :]
            cnt = jnp.minimum(pos, w).astype(_F32)
            p = (ssum[POOL_HALO:, :] / cnt - xg).astype(_BF16)
            q = jnp.dot(p, wp_ref[n], preferred_element_type=_F32) * ps_ref[:, lanes]
            y_ref[:, D_CONV + n * POOL_GROUP:D_CONV + (n + 1) * POOL_GROUP] = q.astype(_BF16)

    glu_block(0)
    conv_block(0)
    glu_block(1)
    pool_branch()
    conv_block(1)
    o_ref[...] = x_ref[...] + jnp.dot(y_ref[:, D_CONV:], wout_ref[D_CONV:, :],
                                      preferred_element_type=_F32)
    for n in range(2, D_CONV // CONV_LANES):
        glu_block(n)
        conv_block(n)

    for r0 in range(0, tm, NORM_ROWS):
        c = c_ref[r0:r0 + NORM_ROWS, :] + cb_ref[...]
        mu = jnp.mean(c, axis=-1, keepdims=True)
        xc = c - mu
        var = jnp.mean(xc * xc, axis=-1, keepdims=True)
        ln = xc * lax.rsqrt(var + EPS) * lng_ref[...] + lnb_ref[...]
        y_ref[r0:r0 + NORM_ROWS, 0:D_CONV] = (ln * _sigmoid(ln)).astype(_BF16)

    o_ref[...] += jnp.dot(y_ref[:, 0:D_CONV], wout_ref[0:D_CONV, :],
                          preferred_element_type=_F32)

    ext_a[0:CONV_HALO, :] = ext_a[tm:tm + CONV_HALO, :]
    ext_b[0:POOL_HALO, :] = ext_b[tm:tm + POOL_HALO, :]


def _mix_e_call(x, norm_g, w_in, conv_w, conv_b, ln_g, ln_b, w_pool, pool_scale, w_out,
                cast_weights):
    b, s, d = x.shape
    tm = TM_MIX_E
    ns = s // tm
    casts = [_cast_blocks(w, layer, b * ns, lambda i, j: i * ns + j)
             for w, layer in cast_weights]
    out, *cast = pl.pallas_call(
        functools.partial(_mix_e_kernel, n_cast=len(casts), tm=tm),
        grid=(b, ns),
        in_specs=[
            pl.BlockSpec((None, tm, d), lambda i, j: (i, j, 0)),
            _resident((1, d)),
            _resident(w_in.shape),
            _resident(conv_w.shape),
            _resident((1, D_CONV)),
            _resident((1, D_CONV)),
            _resident((1, D_CONV)),
            _resident(w_pool.shape),
            _resident((1, D_POOL)),
            _resident(w_out.shape),
        ] + [c.in_spec for c in casts],
        out_specs=[pl.BlockSpec((None, tm, d), lambda i, j: (i, j, 0))]
        + [c.out_spec for c in casts],
        out_shape=[jax.ShapeDtypeStruct((b, s, d), _F32)] + [c.out_shape for c in casts],
        scratch_shapes=[
            pltpu.VMEM((CONV_HALO + tm, D_CONV), _F32),
            pltpu.VMEM((POOL_HALO + tm, D_POOL), _F32),
            pltpu.VMEM((tm, D_CONV), _F32),
            pltpu.VMEM((tm, D_CONV + D_POOL), _BF16),
        ],
        compiler_params=pltpu.CompilerParams(
            dimension_semantics=("arbitrary", "arbitrary"),
            vmem_limit_bytes=VMEM_LIMIT),
        name="mix_e",
    )(x, norm_g.reshape(1, d), w_in, conv_w, conv_b.reshape(1, -1), ln_g.reshape(1, -1),
      ln_b.reshape(1, -1), w_pool, pool_scale.reshape(1, -1), w_out,
      *[w for w, _ in cast_weights])
    return out, cast


def _mix_o_kernel(*refs, n_cast, tm):
    x_ref, g_ref, wb_ref, wc_ref, wv_ref, cw_ref, wout_ref = refs[:7]
    cast_in = refs[7:7 + n_cast]
    o_ref = refs[7 + n_cast]
    cast_out = refs[8 + n_cast:8 + 2 * n_cast]
    hn_ref, hist_ref, ext_ref = refs[8 + 2 * n_cast:]
    s = pl.program_id(1)
    j = pl.program_id(2)

    @pl.when(j == 0)
    def _():
        x = x_ref[...]
        o_ref[...] = x
        hn_ref[...] = _rms(x, g_ref[...]).astype(_BF16)

    @pl.when(s == 0)
    def _():
        hist_ref[j] = jnp.zeros(hist_ref.shape[1:], _F32)

    _cast_side_job(cast_in, cast_out)
    hn = hn_ref[...]
    gate_c = jnp.dot(hn, wc_ref[...], preferred_element_type=_F32)
    v = jnp.dot(hn, wv_ref[...], preferred_element_type=_F32)
    gate_b = jnp.dot(hn, wb_ref[...], preferred_element_type=_F32)
    cv = gate_c * v
    ext_ref[0:SHORT_HALO, :] = hist_ref[j]
    ext_ref[SHORT_HALO:SHORT_HALO + tm, :] = cv
    conv = cw_ref[SHORT_WIDTH - 1:SHORT_WIDTH, :] * cv
    for k in range(SHORT_WIDTH - 1):
        off = SHORT_HALO - (SHORT_WIDTH - 1) + k
        conv = conv + cw_ref[k:k + 1, :] * ext_ref[off:off + tm, :]
    hist_ref[j] = ext_ref[tm:tm + SHORT_HALO, :]
    y = (gate_b * conv).astype(_BF16)
    o_ref[...] += jnp.dot(y, wout_ref[...], preferred_element_type=_F32)


def _mix_o_call(x, norm_g, w_in, conv_w, w_out, cast_weights):
    b, s, d = x.shape
    tm, tc = TM_MIX_O, TC_MIX_O
    nc = D_SHORT // tc
    ns = s // tm
    casts = [_cast_blocks(w, layer, b * ns * nc, lambda i, j, c: (i * ns + j) * nc + c)
             for w, layer in cast_weights]
    out, *cast = pl.pallas_call(
        functools.partial(_mix_o_kernel, n_cast=len(casts), tm=tm),
        grid=(b, ns, nc),
        in_specs=[
            pl.BlockSpec((None, tm, d), lambda i, j, c: (i, j, 0)),
            _resident((1, d)),
            pl.BlockSpec((d, tc), lambda i, j, c: (0, c)),
            pl.BlockSpec((d, tc), lambda i, j, c: (0, nc + c)),
            pl.BlockSpec((d, tc), lambda i, j, c: (0, 2 * nc + c)),
            pl.BlockSpec((SHORT_WIDTH, tc), lambda i, j, c: (0, c)),
            pl.BlockSpec((tc, d), lambda i, j, c: (c, 0)),
        ] + [c.in_spec for c in casts],
        out_specs=[pl.BlockSpec((None, tm, d), lambda i, j, c: (i, j, 0))]
        + [c.out_spec for c in casts],
        out_shape=[jax.ShapeDtypeStruct((b, s, d), _F32)] + [c.out_shape for c in casts],
        scratch_shapes=[
            pltpu.VMEM((tm, d), _BF16),
            pltpu.VMEM((nc, SHORT_HALO, tc), _F32),
            pltpu.VMEM((SHORT_HALO + tm, tc), _F32),
        ],
        compiler_params=pltpu.CompilerParams(
            dimension_semantics=("arbitrary", "arbitrary", "arbitrary"),
            vmem_limit_bytes=VMEM_LIMIT),
        name="mix_o",
    )(x, norm_g.reshape(1, d), w_in, w_in, w_in, conv_w, w_out, *[w for w, _ in cast_weights])
    return out, cast


def _ffn_kernel(*refs, n_cast, final_norm):
    x_ref, g_ref, wg_ref, wu_ref, wd_ref, fg_ref = refs[:6]
    cast_in = refs[6:6 + n_cast]
    o_ref = refs[6 + n_cast]
    cast_out = refs[7 + n_cast:7 + 2 * n_cast]
    (hn_ref,) = refs[7 + 2 * n_cast:]
    f = pl.program_id(1)

    @pl.when(f == 0)
    def _():
        h = x_ref[...]
        o_ref[...] = h
        hn_ref[...] = _rms(h, g_ref[...]).astype(_BF16)

    _cast_side_job(cast_in, cast_out)
    hn = hn_ref[...]
    half = wg_ref.shape[1] // 2
    zs = []
    for h in range(2):
        cols = slice(h * half, (h + 1) * half)
        gate = jnp.dot(hn, wg_ref[:, cols], preferred_element_type=_F32)
        up = jnp.dot(hn, wu_ref[:, cols], preferred_element_type=_F32)
        zs.append((gate * _sigmoid(gate) * up).astype(_BF16))
    for h in range(2):
        o_ref[...] += jnp.dot(zs[h], wd_ref[h * half:(h + 1) * half, :],
                              preferred_element_type=_F32)

    if final_norm:
        @pl.when(f == pl.num_programs(1) - 1)
        def _():
            o_ref[...] = _rms(o_ref[...], fg_ref[...])


def _ffn_call(x, norm_g, w_gate, w_up, w_down, final_g, cast_weights, *, final_norm):
    b, s, d = x.shape
    t = b * s
    tm, tf = TM_FFN, TF_FFN
    nf = D_FF // tf
    casts = [_cast_blocks(w, layer, (t // tm) * nf, lambda i, f: i * nf + f)
             for w, layer in cast_weights]
    out, *cast = pl.pallas_call(
        functools.partial(_ffn_kernel, n_cast=len(casts), final_norm=final_norm),
        grid=(t // tm, nf),
        in_specs=[
            pl.BlockSpec((tm, d), lambda i, f: (i, 0)),
            _resident((1, d)),
            pl.BlockSpec((d, tf), lambda i, f: (0, f)),
            pl.BlockSpec((d, tf), lambda i, f: (0, f)),
            pl.BlockSpec((tf, d), lambda i, f: (f, 0)),
            _resident((1, d)),
        ] + [c.in_spec for c in casts],
        out_specs=[pl.BlockSpec((tm, d), lambda i, f: (i, 0))] + [c.out_spec for c in casts],
        out_shape=[jax.ShapeDtypeStruct((t, d), _F32)] + [c.out_shape for c in casts],
        scratch_shapes=[pltpu.VMEM((tm, d), _BF16)],
        compiler_params=pltpu.CompilerParams(
            dimension_semantics=("arbitrary", "arbitrary"),
            vmem_limit_bytes=VMEM_LIMIT),
        name="ffn_final" if final_norm else "ffn",
    )(x.reshape(t, d), norm_g.reshape(1, d), w_gate, w_up, w_down, final_g.reshape(1, d),
      *[w for w, _ in cast_weights])
    return out.reshape(b, s, d), cast


def kernel(x, mix_norm_e, w_in_e, conv_w_e, conv_b_e, ln_g_e, ln_b_e, w_pool_e, pool_scale_e,
           w_out_e, mix_norm_o, w_in_o, conv_w_o, w_out_o, ffn_norm, w_gate, w_up, w_down,
           final_norm):
    bf = lambda w: w.astype(_BF16)
    h, (wg0, wu0, wd0, wi1, wo1) = _mix_e_call(
        x, mix_norm_e[0], bf(w_in_e[0]), conv_w_e[0], conv_b_e[0], ln_g_e[0], ln_b_e[0],
        bf(w_pool_e[0]), pool_scale_e[0], bf(w_out_e[0]),
        cast_weights=[(w_gate, 0), (w_up, 0), (w_down, 0), (w_in_o, 0), (w_out_o, 0)])
    h, _ = _ffn_call(h, ffn_norm[0], wg0, wu0, wd0, final_norm, cast_weights=[],
                     final_norm=False)
    h, (wg1, wu1, wd1) = _mix_o_call(h, mix_norm_o[0], wi1, conv_w_o[0], wo1,
                                     cast_weights=[(w_gate, 1), (w_up, 1), (w_down, 1)])
    out, _ = _ffn_call(h, ffn_norm[1], wg1, wu1, wd1, final_norm, cast_weights=[],
                       final_norm=True)
    return out
```

```python
import functools
from typing import NamedTuple

import jax
import jax.numpy as jnp
from jax import lax
from jax.experimental import pallas as pl
from jax.experimental.pallas import tpu as pltpu

D_MODEL = 2048
D_CONV = D_MODEL // 2
CONV_WIDTH = 31
D_POOL = D_MODEL // 2
POOL_WINDOWS = (2, 4, 8, 16)
POOL_GROUP = D_POOL // len(POOL_WINDOWS)
D_SHORT = D_MODEL
SHORT_WIDTH = 3
D_FF = -(-8 * D_MODEL // (3 * 256)) * 256
EPS = 1e-6

SUBLANES = 8
LANES = 128
BF16_ROWS = 16
CONV_HALO = 32
POOL_HALO = 16
SHORT_HALO = 8

TM_MIX_E = 256
TM_MIX_O = 512
TC_MIX_O = 512
TM_FFN = 1024
TF_FFN = 512
CONV_ROWS = 64
CONV_LANES = POOL_GROUP
NORM_ROWS = 32
VMEM_LIMIT = 58 * 1024 * 1024

_F32 = jnp.float32
_BF16 = jnp.bfloat16


def _rms(xf, g):
    return xf * lax.rsqrt(jnp.mean(xf * xf, axis=-1, keepdims=True) + EPS) * g


def _sigmoid(x):
    return 1.0 / (1.0 + jnp.exp(-x))


def _resident(shape):
    nd = len(shape)
    return pl.BlockSpec(shape, lambda *_: (0,) * nd, pipeline_mode=pl.Buffered(1))


class _CastBlocks(NamedTuple):
    in_spec: pl.BlockSpec
    out_spec: pl.BlockSpec
    out_shape: jax.ShapeDtypeStruct


def _cast_blocks(w, layer, n_steps, step_of):
    _, rows, cols = w.shape
    blk = next(r for r in range(BF16_ROWS, rows + 1, BF16_ROWS)
               if rows % r == 0 and rows // r <= n_steps)
    last = rows // blk - 1
    return _CastBlocks(
        pl.BlockSpec((None, blk, cols), lambda *g: (layer, jnp.minimum(step_of(*g), last), 0)),
        pl.BlockSpec((blk, cols), lambda *g: (jnp.minimum(step_of(*g), last), 0)),
        jax.ShapeDtypeStruct((rows, cols), _BF16))


def _cast_side_job(cast_in, cast_out):
    for src, dst in zip(cast_in, cast_out):
        dst[...] = src[...].astype(_BF16)


def _mix_e_kernel(*refs, n_cast, tm):
    (x_ref, g_ref, win_ref, cw_ref, cb_ref, lng_ref, lnb_ref, wp_ref, ps_ref,
     wout_ref) = refs[:10]
    cast_in = refs[10:10 + n_cast]
    o_ref = refs[10 + n_cast]
    cast_out = refs[11 + n_cast:11 + 2 * n_cast]
    ext_a, ext_b, c_ref, y_ref = refs[11 + 2 * n_cast:]
    s = pl.program_id(1)

    @pl.when(s == 0)
    def _():
        ext_a[0:CONV_HALO, :] = jnp.zeros((CONV_HALO, D_CONV), _F32)
        ext_b[0:POOL_HALO, :] = jnp.zeros((POOL_HALO, D_POOL), _F32)

    _cast_side_job(cast_in, cast_out)
    hn = _rms(x_ref[...], g_ref[...]).astype(_BF16)
    pos = s * tm + lax.broadcasted_iota(jnp.int32, (tm, 1), 0) + 1

    slab_rows = CONV_ROWS + CONV_HALO
    first_off = CONV_HALO - (CONV_WIDTH - 1)

    def glu_block(n):
        for h in range(CONV_LANES // LANES):
            c0 = n * CONV_LANES + h * LANES
            w_vg = jnp.concatenate([win_ref[:, c0:c0 + LANES],
                                    win_ref[:, D_CONV + c0:D_CONV + c0 + LANES]], axis=1)
            u = jnp.dot(hn, w_vg, preferred_element_type=_F32)
            ext_a[CONV_HALO:CONV_HALO + tm, c0:c0 + LANES] = u[:, :LANES] * _sigmoid(u[:, LANES:])

    def conv_block(n):
        for h in range(CONV_LANES // LANES):
            cl = slice(n * CONV_LANES + h * LANES, n * CONV_LANES + (h + 1) * LANES)
            for r0 in range(0, tm, CONV_ROWS):
                slab = ext_a[r0:r0 + slab_rows, cl]
                acc = jnp.zeros((CONV_ROWS, LANES), _F32)
                for r in range(SUBLANES):
                    rot = slab if r == 0 else pltpu.roll(slab, slab_rows - r, axis=0)
                    for q in range(CONV_HALO // SUBLANES + 1):
                        j = SUBLANES * q + r - first_off
                        if 0 <= j < CONV_WIDTH:
                            tap = rot[SUBLANES * q:SUBLANES * q + CONV_ROWS, :]
                            acc = acc + cw_ref[j:j + 1, cl] * tap
                c_ref[r0:r0 + CONV_ROWS, cl] = acc

    def pool_branch():
        ext_b[POOL_HALO:POOL_HALO + tm, :] = jnp.dot(hn, win_ref[:, 2 * D_CONV:],
                                                     preferred_element_type=_F32)
        for n, w in enumerate(POOL_WINDOWS):
            lanes = slice(n * POOL_GROUP, (n + 1) * POOL_GROUP)
            xe = ext_b[:, lanes]
            ssum = xe
            span = 1
            while span < w:
                ssum = ssum + pltpu.roll(ssum, span, axis=0)
                span *= 2
            xg = xe[POOL_HALO:, :]
            cnt = jnp.minimum(pos, w).astype(_F32)
            p = (ssum[POOL_HALO:, :] / cnt - xg).astype(_BF16)
            q = jnp.dot(p, wp_ref[n], preferred_element_type=_F32) * ps_ref[:, lanes]
            y_ref[:, D_CONV + n * POOL_GROUP:D_CONV + (n + 1) * POOL_GROUP] = q.astype(_BF16)

    glu_block(0)
    conv_block(0)
    glu_block(1)
    conv_block(1)
    glu_block(2)
    pool_branch()
    conv_block(2)
    o_ref[...] = x_ref[...] + jnp.dot(y_ref[:, D_CONV:], wout_ref[D_CONV:, :],
                                      preferred_element_type=_F32)
    glu_block(3)
    conv_block(3)

    for r0 in range(0, tm, NORM_ROWS):
        c = c_ref[r0:r0 + NORM_ROWS, :] + cb_ref[...]
        mu = jnp.mean(c, axis=-1, keepdims=True)
        xc = c - mu
        var = jnp.mean(xc * xc, axis=-1, keepdims=True)
        ln = xc * lax.rsqrt(var + EPS) * lng_ref[...] + lnb_ref[...]
        y_ref[r0:r0 + NORM_ROWS, 0:D_CONV] = (ln * _sigmoid(ln)).astype(_BF16)

    o_ref[...] += jnp.dot(y_ref[:, 0:D_CONV], wout_ref[0:D_CONV, :],
                          preferred_element_type=_F32)

    ext_a[0:CONV_HALO, :] = ext_a[tm:tm + CONV_HALO, :]
    ext_b[0:POOL_HALO, :] = ext_b[tm:tm + POOL_HALO, :]


def _mix_e_call(x, norm_g, w_in, conv_w, conv_b, ln_g, ln_b, w_pool, pool_scale, w_out,
                cast_weights):
    b, s, d = x.shape
    tm = TM_MIX_E
    ns = s // tm
    casts = [_cast_blocks(w, layer, b * ns, lambda i, j: i * ns + j)
             for w, layer in cast_weights]
    out, *cast = pl.pallas_call(
        functools.partial(_mix_e_kernel, n_cast=len(casts), tm=tm),
        grid=(b, ns),
        in_specs=[
            pl.BlockSpec((None, tm, d), lambda i, j: (i, j, 0)),
            _resident((1, d)),
            _resident(w_in.shape),
            _resident(conv_w.shape),
            _resident((1, D_CONV)),
            _resident((1, D_CONV)),
            _resident((1, D_CONV)),
            _resident(w_pool.shape),
            _resident((1, D_POOL)),
            _resident(w_out.shape),
        ] + [c.in_spec for c in casts],
        out_specs=[pl.BlockSpec((None, tm, d), lambda i, j: (i, j, 0))]
        + [c.out_spec for c in casts],
        out_shape=[jax.ShapeDtypeStruct((b, s, d), _F32)] + [c.out_shape for c in casts],
        scratch_shapes=[
            pltpu.VMEM((CONV_HALO + tm, D_CONV), _F32),
            pltpu.VMEM((POOL_HALO + tm, D_POOL), _F32),
            pltpu.VMEM((tm, D_CONV), _F32),
            pltpu.VMEM((tm, D_CONV + D_POOL), _BF16),
        ],
        compiler_params=pltpu.CompilerParams(
            dimension_semantics=("arbitrary", "arbitrary"),
            vmem_limit_bytes=VMEM_LIMIT),
        name="mix_e",
    )(x, norm_g.reshape(1, d), w_in, conv_w, conv_b.reshape(1, -1), ln_g.reshape(1, -1),
      ln_b.reshape(1, -1), w_pool, pool_scale.reshape(1, -1), w_out,
      *[w for w, _ in cast_weights])
    return out, cast


def _mix_o_kernel(*refs, n_cast, tm):
    x_ref, g_ref, wb_ref, wc_ref, wv_ref, cw_ref, wout_ref = refs[:7]
    cast_in = refs[7:7 + n_cast]
    o_ref = refs[7 + n_cast]
    cast_out = refs[8 + n_cast:8 + 2 * n_cast]
    hn_ref, hist_ref, ext_ref = refs[8 + 2 * n_cast:]
    s = pl.program_id(1)
    j = pl.program_id(2)

    @pl.when(j == 0)
    def _():
        x = x_ref[...]
        o_ref[...] = x
        hn_ref[...] = _rms(x, g_ref[...]).astype(_BF16)

    @pl.when(s == 0)
    def _():
        hist_ref[j] = jnp.zeros(hist_ref.shape[1:], _F32)

    _cast_side_job(cast_in, cast_out)
    hn = hn_ref[...]
    gate_c = jnp.dot(hn, wc_ref[...], preferred_element_type=_F32)
    v = jnp.dot(hn, wv_ref[...], preferred_element_type=_F32)
    gate_b = jnp.dot(hn, wb_ref[...], preferred_element_type=_F32)
    cv = gate_c * v
    ext_ref[0:SHORT_HALO, :] = hist_ref[j]
    ext_ref[SHORT_HALO:SHORT_HALO + tm, :] = cv
    conv = cw_ref[SHORT_WIDTH - 1:SHORT_WIDTH, :] * cv
    for k in range(SHORT_WIDTH - 1):
        off = SHORT_HALO - (SHORT_WIDTH - 1) + k
        conv = conv + cw_ref[k:k + 1, :] * ext_ref[off:off + tm, :]
    hist_ref[j] = ext_ref[tm:tm + SHORT_HALO, :]
    y = (gate_b * conv).astype(_BF16)
    o_ref[...] += jnp.dot(y, wout_ref[...], preferred_element_type=_F32)


def _mix_o_call(x, norm_g, w_in, conv_w, w_out, cast_weights):
    b, s, d = x.shape
    tm, tc = TM_MIX_O, TC_MIX_O
    nc = D_SHORT // tc
    ns = s // tm
    casts = [_cast_blocks(w, layer, b * ns * nc, lambda i, j, c: (i * ns + j) * nc + c)
             for w, layer in cast_weights]
    out, *cast = pl.pallas_call(
        functools.partial(_mix_o_kernel, n_cast=len(casts), tm=tm),
        grid=(b, ns, nc),
        in_specs=[
            pl.BlockSpec((None, tm, d), lambda i, j, c: (i, j, 0)),
            _resident((1, d)),
            pl.BlockSpec((d, tc), lambda i, j, c: (0, c)),
            pl.BlockSpec((d, tc), lambda i, j, c: (0, nc + c)),
            pl.BlockSpec((d, tc), lambda i, j, c: (0, 2 * nc + c)),
            pl.BlockSpec((SHORT_WIDTH, tc), lambda i, j, c: (0, c)),
            pl.BlockSpec((tc, d), lambda i, j, c: (c, 0)),
        ] + [c.in_spec for c in casts],
        out_specs=[pl.BlockSpec((None, tm, d), lambda i, j, c: (i, j, 0))]
        + [c.out_spec for c in casts],
        out_shape=[jax.ShapeDtypeStruct((b, s, d), _F32)] + [c.out_shape for c in casts],
        scratch_shapes=[
            pltpu.VMEM((tm, d), _BF16),
            pltpu.VMEM((nc, SHORT_HALO, tc), _F32),
            pltpu.VMEM((SHORT_HALO + tm, tc), _F32),
        ],
        compiler_params=pltpu.CompilerParams(
            dimension_semantics=("arbitrary", "arbitrary", "arbitrary"),
            vmem_limit_bytes=VMEM_LIMIT),
        name="mix_o",
    )(x, norm_g.reshape(1, d), w_in, w_in, w_in, conv_w, w_out, *[w for w, _ in cast_weights])
    return out, cast


def _ffn_kernel(*refs, n_cast, final_norm):
    x_ref, g_ref, wg_ref, wu_ref, wd_ref, fg_ref = refs[:6]
    cast_in = refs[6:6 + n_cast]
    o_ref = refs[6 + n_cast]
    cast_out = refs[7 + n_cast:7 + 2 * n_cast]
    (hn_ref,) = refs[7 + 2 * n_cast:]
    f = pl.program_id(1)

    @pl.when(f == 0)
    def _():
        h = x_ref[...]
        o_ref[...] = h
        hn_ref[...] = _rms(h, g_ref[...]).astype(_BF16)

    _cast_side_job(cast_in, cast_out)
    hn = hn_ref[...]
    half = wg_ref.shape[1] // 2
    zs = []
    for h in range(2):
        cols = slice(h * half, (h + 1) * half)
        gate = jnp.dot(hn, wg_ref[:, cols], preferred_element_type=_F32)
        up = jnp.dot(hn, wu_ref[:, cols], preferred_element_type=_F32)
        zs.append((gate * _sigmoid(gate) * up).astype(_BF16))
    for h in range(2):
        o_ref[...] += jnp.dot(zs[h], wd_ref[h * half:(h + 1) * half, :],
                              preferred_element_type=_F32)

    if final_norm:
        @pl.when(f == pl.num_programs(1) - 1)
        def _():
            o_ref[...] = _rms(o_ref[...], fg_ref[...])


def _ffn_call(x, norm_g, w_gate, w_up, w_down, final_g, cast_weights, *, final_norm):
    b, s, d = x.shape
    t = b * s
    tm, tf = TM_FFN, TF_FFN
    nf = D_FF // tf
    casts = [_cast_blocks(w, layer, (t // tm) * nf, lambda i, f: i * nf + f)
             for w, layer in cast_weights]
    out, *cast = pl.pallas_call(
        functools.partial(_ffn_kernel, n_cast=len(casts), final_norm=final_norm),
        grid=(t // tm, nf),
        in_specs=[
            pl.BlockSpec((tm, d), lambda i, f: (i, 0)),
            _resident((1, d)),
            pl.BlockSpec((d, tf), lambda i, f: (0, f)),
            pl.BlockSpec((d, tf), lambda i, f: (0, f)),
            pl.BlockSpec((tf, d), lambda i, f: (f, 0)),
            _resident((1, d)),
        ] + [c.in_spec for c in casts],
        out_specs=[pl.BlockSpec((tm, d), lambda i, f: (i, 0))] + [c.out_spec for c in casts],
        out_shape=[jax.ShapeDtypeStruct((t, d), _F32)] + [c.out_shape for c in casts],
        scratch_shapes=[pltpu.VMEM((tm, d), _BF16)],
        compiler_params=pltpu.CompilerParams(
            dimension_semantics=("arbitrary", "arbitrary"),
            vmem_limit_bytes=VMEM_LIMIT),
        name="ffn_final" if final_norm else "ffn",
    )(x.reshape(t, d), norm_g.reshape(1, d), w_gate, w_up, w_down, final_g.reshape(1, d),
      *[w for w, _ in cast_weights])
    return out.reshape(b, s, d), cast


def kernel(x, mix_norm_e, w_in_e, conv_w_e, conv_b_e, ln_g_e, ln_b_e, w_pool_e, pool_scale_e,
           w_out_e, mix_norm_o, w_in_o, conv_w_o, w_out_o, ffn_norm, w_gate, w_up, w_down,
           final_norm):
    bf = lambda w: w.astype(_BF16)
    h, (wg0, wu0, wd0, wi1, wo1) = _mix_e_call(
        x, mix_norm_e[0], bf(w_in_e[0]), conv_w_e[0], conv_b_e[0], ln_g_e[0], ln_b_e[0],
        bf(w_pool_e[0]), pool_scale_e[0], bf(w_out_e[0]),
        cast_weights=[(w_gate, 0), (w_up, 0), (w_down, 0), (w_in_o, 0), (w_out_o, 0)])
    h, _ = _ffn_call(h, ffn_norm[0], wg0, wu0, wd0, final_norm, cast_weights=[],
                     final_norm=False)
    h, (wg1, wu1, wd1) = _mix_o_call(h, mix_norm_o[0], wi1, conv_w_o[0], wo1,
                                     cast_weights=[(w_gate, 1), (w_up, 1), (w_down, 1)])
    out, _ = _ffn_call(h, ffn_norm[1], wg1, wu1, wd1, final_norm, cast_weights=[],
                       final_norm=True)
    return out
```
